```python
import jax
import jax.numpy as jnp
from jax import lax
import numpy as np

D_MODEL = 1024
BATCH = 4
SEQ = 4096
DEPTH = 2

HEAD_DIM = 64
BLOCK = 128
NEG_INF = -1e30

A_HEADS = 8
A_KV_HEADS = 2
A_WINDOW = 128
B_HEADS = 8
B_PATTERNS = ((128, 1), (512, 4), (2048, 16))
C_HEADS = 8
IDX_HEADS = 4
IDX_DIM = 64
TOPK_MAX = 256
D_HEADS = 8

N_BRANCH = 4
BRANCH_WIDTH = 8 * HEAD_DIM
N_ALIBI_HEADS = A_HEADS + B_HEADS + C_HEADS

D_FF = 2816
CONV_WIDTH = 3
LN_EPS = 1e-5
DEEPNORM_ALPHA = (2 * DEPTH) ** 0.25
DEEPNORM_BETA = (8 * DEPTH) ** -0.25

IN_SPLIT_SIZES = (
    A_HEADS * HEAD_DIM, A_KV_HEADS * HEAD_DIM, A_KV_HEADS * HEAD_DIM,
    B_HEADS * HEAD_DIM, B_HEADS * HEAD_DIM, B_HEADS * HEAD_DIM,
    C_HEADS * HEAD_DIM, HEAD_DIM, HEAD_DIM, IDX_HEADS * IDX_DIM, IDX_DIM, IDX_HEADS,
    D_HEADS * HEAD_DIM, D_HEADS * HEAD_DIM, D_HEADS * HEAD_DIM, D_HEADS,
)
IN_WIDTH = sum(IN_SPLIT_SIZES)

kernel_name = "hybrid_gated_parallel_mixer_deepnorm"


def layer_norm(x, g, b):
    xf = x.astype(jnp.float32)
    mu = jnp.mean(xf, axis=-1, keepdims=True)
    var = jnp.mean(jnp.square(xf - mu), axis=-1, keepdims=True)
    y = (xf - mu) * lax.rsqrt(var + LN_EPS) * g.astype(jnp.float32) + b.astype(jnp.float32)
    return y.astype(x.dtype)


def alibi_slopes():
    s = jnp.exp2(-8.0 * jnp.arange(1, N_ALIBI_HEADS + 1, dtype=jnp.float32) / N_ALIBI_HEADS)
    return s[0::3], s[1::3], s[2::3]


def banded_attention(q, k, v, slopes, max_dist, dist_scale, sinks=None):
    n, L, hq, hd = q.shape
    hkv = k.shape[2]
    g = hq // hkv
    nb = -(-L // BLOCK)
    pad = nb * BLOCK - L
    if pad:
        cfg = ((0, 0), (0, pad), (0, 0), (0, 0))
        q, k, v = jnp.pad(q, cfg), jnp.pad(k, cfg), jnp.pad(v, cfg)
    qb = q.reshape(n, nb, BLOCK, hkv, g, hd)
    kb = k.reshape(n, nb, BLOCK, hkv, hd)
    vb = v.reshape(n, nb, BLOCK, hkv, hd)
    kk = jnp.concatenate([jnp.concatenate([jnp.zeros_like(kb[:, :1]), kb[:, :-1]], axis=1), kb], axis=2)
    vv = jnp.concatenate([jnp.concatenate([jnp.zeros_like(vb[:, :1]), vb[:, :-1]], axis=1), vb], axis=2)
    s = jnp.einsum('nbqhgd,nbkhd->nbhgqk', qb, kk).astype(jnp.float32) * (hd ** -0.5)
    qi = jnp.arange(BLOCK)[:, None]
    ki = jnp.arange(2 * BLOCK)[None, :]
    dist = qi - ki + BLOCK
    first = (jnp.arange(nb) == 0)[:, None, None]
    valid = (dist >= 0) & (dist <= max_dist) & ~(first & (ki < BLOCK))
    s = s - slopes.reshape(hkv, g)[:, :, None, None] * (dist * dist_scale).astype(jnp.float32)
    s = jnp.where(valid[None, :, None, None], s, NEG_INF)
    m = jnp.max(s, axis=-1)
    if sinks is not None:
        sk = sinks.astype(jnp.float32).reshape(hkv, g)[:, :, None]
        m = jnp.maximum(m, sk)
    p = jnp.exp(s - m[..., None])
    l = jnp.sum(p, axis=-1)
    if sinks is not None:
        l = l + jnp.exp(sk - m)
    p = p / l[..., None]
    o = jnp.einsum('nbhgqk,nbkhd->nbqhgd', p.astype(v.dtype), vv).reshape(n, nb * BLOCK, hq, hd)[:, :L]
    m = m.transpose(0, 1, 4, 2, 3).reshape(n, nb * BLOCK, hq)[:, :L]
    l = l.transpose(0, 1, 4, 2, 3).reshape(n, nb * BLOCK, hq)[:, :L]
    return o, m, l


def dilated_attention(q, k, v, slopes):
    b, L, h, hd = q.shape
    outs, ms, ls = [], [], []
    for window, dil in B_PATTERNS:
        to_cls = lambda t: t.reshape(b, L // dil, dil, h, hd).transpose(0, 2, 1, 3, 4).reshape(b * dil, L // dil, h, hd)
        o, m, l = banded_attention(to_cls(q), to_cls(k), to_cls(v), slopes, window // dil, dil)
        outs.append(o.reshape(b, dil, L // dil, h, hd).transpose(0, 2, 1, 3, 4).reshape(b, L, h, hd))
        ms.append(m.reshape(b, dil, L // dil, h).transpose(0, 2, 1, 3).reshape(b, L, h))
        ls.append(l.reshape(b, dil, L // dil, h).transpose(0, 2, 1, 3).reshape(b, L, h))
    ms = jnp.stack(ms)
    w = jnp.stack(ls) * jnp.exp(ms - jnp.max(ms, axis=0, keepdims=True))
    o = jnp.sum(w[..., None] * jnp.stack(outs).astype(jnp.float32), axis=0) / jnp.sum(w, axis=0)[..., None]
    return o.astype(q.dtype)


def dsa_attention(q, k, v, iq, ik, iw, slopes, topk):
    b, L, h, hd = q.shape
    nb = L // BLOCK
    bidx = jnp.arange(b)[:, None, None]
    kpos = jnp.arange(L)

    def block(i):
        start = i * BLOCK
        qb = lax.dynamic_slice_in_dim(q, start, BLOCK, axis=1)
        iqb = lax.dynamic_slice_in_dim(iq, start, BLOCK, axis=1)
        iwb = lax.dynamic_slice_in_dim(iw, start, BLOCK, axis=1)
        qpos = start + jnp.arange(BLOCK)
        rel = jax.nn.relu(jnp.einsum('bqhd,bsd->bqhs', iqb, ik).astype(jnp.float32) * (IDX_DIM ** -0.5))
        score = jnp.einsum('bqh,bqhs->bqs', iwb.astype(jnp.float32), rel)
        score = jnp.where((kpos[None, :] <= qpos[:, None])[None], score, NEG_INF)
        _, idx = lax.top_k(score, topk)
        valid = idx <= qpos[None, :, None]
        kg = k[bidx, idx]
        vg = v[bidx, idx]
        s = jnp.einsum('bqhd,bqkd->bhqk', qb, kg).astype(jnp.float32) * (hd ** -0.5)
        dist = (qpos[None, :, None] - idx).astype(jnp.float32)
        s = s - slopes[None, :, None, None] * dist[:, None]
        s = jnp.where(valid[:, None], s, NEG_INF)
        p = jax.nn.softmax(s, axis=-1)
        return jnp.einsum('bhqk,bqkd->bqhd', p.astype(v.dtype), vg)

    o = lax.map(block, jnp.arange(nb))
    return o.transpose(1, 0, 2, 3, 4).reshape(b, L, h, hd)


def forgetting_attention(q, k, v, f_logit):
    b, L, h, hd = q.shape
    nb = L // BLOCK
    c = jnp.cumsum(jax.nn.log_sigmoid(f_logit.astype(jnp.float32)), axis=1).transpose(0, 2, 1)
    kpos = jnp.arange(L)

    def block(i):
        start = i * BLOCK
        qb = lax.dynamic_slice_in_dim(q, start, BLOCK, axis=1)
        cq = lax.dynamic_slice_in_dim(c, start, BLOCK, axis=2)
        qpos = start + jnp.arange(BLOCK)
        s = jnp.einsum('bqhd,bshd->bhqs', qb, k).astype(jnp.float32) * (hd ** -0.5)
        s = s + cq[..., None] - c[:, :, None, :]
        s = jnp.where((kpos[None, :] <= qpos[:, None])[None, None], s, NEG_INF)
        p = jax.nn.softmax(s, axis=-1)
        return jnp.einsum('bhqs,bshd->bqhd', p.astype(v.dtype), v)

    o = lax.map(block, jnp.arange(nb))
    return o.transpose(1, 0, 2, 3, 4).reshape(b, L, h, hd)


def token_mixer(x, w_in, b_forget, sinks, w_branch, w_gate, b_gate, w_out):
    b, L, _ = x.shape
    offsets = np.cumsum(IN_SPLIT_SIZES)[:-1].tolist()
    (aq, ak, av, bq, bk, bv, cq, ck, cv, ciq, cik, ciw, dq, dk, dv, df) = jnp.split(x @ w_in, offsets, axis=-1)
    heads = lambda t, n: t.reshape(b, L, n, HEAD_DIM)
    slopes_a, slopes_b, slopes_c = alibi_slopes()
    topk = min(TOPK_MAX, L // 4)
    o_a, _, _ = banded_attention(heads(aq, A_HEADS), heads(ak, A_KV_HEADS), heads(av, A_KV_HEADS),
                                 slopes_a, A_WINDOW - 1, 1, sinks)
    o_b = dilated_attention(heads(bq, B_HEADS), heads(bk, B_HEADS), heads(bv, B_HEADS), slopes_b)
    o_c = dsa_attention(heads(cq, C_HEADS), ck, cv, ciq.reshape(b, L, IDX_HEADS, IDX_DIM), cik, ciw,
                        slopes_c, topk)
    o_d = forgetting_attention(heads(dq, D_HEADS), heads(dk, D_HEADS), heads(dv, D_HEADS), df + b_forget)
    branches = jnp.stack([o.reshape(b, L, BRANCH_WIDTH) for o in (o_a, o_b, o_c, o_d)], axis=2)
    proj = jnp.einsum('bsnk,nkd->bsnd', branches, w_branch)
    gates = jax.nn.sigmoid(x @ w_gate + b_gate).reshape(b, L, N_BRANCH, D_MODEL)
    merged = jnp.sum(gates * proj, axis=2)
    return merged @ w_out


def conv_glu_ffn(x, w_up, w_ffn_gate, conv_w, conv_b, w_down):
    L = x.shape[1]
    a = x @ w_up
    ap = jnp.pad(a, ((0, 0), (CONV_WIDTH - 1, 0), (0, 0)))
    a = conv_b + sum(conv_w[j] * ap[:, j:j + L] for j in range(CONV_WIDTH))
    return (jax.nn.gelu(a) * (x @ w_ffn_gate)) @ w_down


def setup_inputs(seed: int = 0) -> dict:
    key = jax.random.key(seed)
    ks = jax.random.split(key, 17)
    nrm = lambda k, shape, scale: jax.random.normal(k, shape, jnp.float32) * scale
    return {
        "x": nrm(ks[0], (BATCH, SEQ, D_MODEL), 1.0),
        "w_in": nrm(ks[1], (DEPTH, D_MODEL, IN_WIDTH), D_MODEL ** -0.5),
        "b_forget": jnp.linspace(1.0, 6.0, D_HEADS, dtype=jnp.float32)[None] + nrm(ks[2], (DEPTH, D_HEADS), 0.1),
        "sinks": nrm(ks[3], (DEPTH, A_HEADS), 0.5),
        "w_branch": nrm(ks[4], (DEPTH, N_BRANCH, BRANCH_WIDTH, D_MODEL), BRANCH_WIDTH ** -0.5),
        "w_gate": nrm(ks[5], (DEPTH, D_MODEL, N_BRANCH * D_MODEL), D_MODEL ** -0.5),
        "b_gate": nrm(ks[6], (DEPTH, N_BRANCH * D_MODEL), 0.1),
        "w_out": nrm(ks[7], (DEPTH, D_MODEL, D_MODEL), DEEPNORM_BETA * D_MODEL ** -0.5),
        "ln1_g": 1.0 + nrm(ks[8], (DEPTH, D_MODEL), 0.02),
        "ln1_b": nrm(ks[9], (DEPTH, D_MODEL), 0.02),
        "w_up": nrm(ks[10], (DEPTH, D_MODEL, D_FF), D_MODEL ** -0.5),
        "w_ffn_gate": nrm(ks[11], (DEPTH, D_MODEL, D_FF), D_MODEL ** -0.5),
        "conv_w": nrm(ks[12], (DEPTH, CONV_WIDTH, D_FF), CONV_WIDTH ** -0.5),
        "conv_b": nrm(ks[13], (DEPTH, D_FF), 0.02),
        "w_down": nrm(ks[14], (DEPTH, D_FF, D_MODEL), DEEPNORM_BETA * D_FF ** -0.5),
        "ln2_g": 1.0 + nrm(ks[15], (DEPTH, D_MODEL), 0.02),
        "ln2_b": nrm(ks[16], (DEPTH, D_MODEL), 0.02),
    }


def reference(x, w_in, b_forget, sinks, w_branch, w_gate, b_gate, w_out, ln1_g, ln1_b,
              w_up, w_ffn_gate, conv_w, conv_b, w_down, ln2_g, ln2_b):
    for l in range(DEPTH):
        y = token_mixer(x, w_in[l], b_forget[l], sinks[l], w_branch[l], w_gate[l], b_gate[l], w_out[l])
        x = layer_norm(DEEPNORM_ALPHA * x + y, ln1_g[l], ln1_b[l])
        y = conv_glu_ffn(x, w_up[l], w_ffn_gate[l], conv_w[l], conv_b[l], w_down[l])
        x = layer_norm(DEEPNORM_ALPHA * x + y, ln2_g[l], ln2_b[l])
    return x
```

```python
import functools

import numpy as np
import jax
import jax.numpy as jnp
from jax import lax
from jax.experimental import pallas as pl
from jax.experimental.pallas import tpu as pltpu

F32 = jnp.float32
BF16 = jnp.bfloat16
I32 = jnp.int32

D_MODEL = 1024
HEAD_DIM = 64
N_HEADS = 8
BRANCH_WIDTH = N_HEADS * HEAD_DIM
BLOCK = 128
NEG_INF = -1e30
INT_MIN = -2147483648
A_WINDOW = 128
B_PATTERNS = ((128, 1), (512, 4), (2048, 16))
IDX_HEADS = 4
TOPK_MAX = 256
D_FF = 2816
LN_EPS = 1e-5
DEPTH = 2
DEEPNORM_ALPHA = (2 * DEPTH) ** 0.25
IN_SPLIT_SIZES = (512, 128, 128, 512, 512, 512, 512, 64, 64, 256, 64, 4, 512, 512, 512, 8)

A_ORDER = (0, 4, 1, 5, 2, 6, 3, 7)
PLAIN_ORDER = tuple(range(N_HEADS))

COL_AQ, COL_BQ, COL_BK, COL_BV, COL_CQ, COL_DQ, COL_DK, COL_DV = (i * 512 for i in range(8))
COL_CIQ = 4096
COL_AK, COL_AV, COL_CK, COL_CV, COL_CIK = 4352, 4480, 4608, 4736, 4864
PROJ_WIDTH = 5120
MISC_ROWS = 16

V7X_VMEM_BYTES = 64 * 2**20


def _alibi_slopes():
    s = np.exp2(-8.0 * np.arange(1, 25, dtype=np.float32) / 24).astype(np.float32)
    return [float(v) for v in s[0::3]], [float(v) for v in s[1::3]], [float(v) for v in s[2::3]]


SLOPES_A, SLOPES_B, SLOPES_C = _alibi_slopes()


def _params(semantics, vmem_mib):
    return pltpu.CompilerParams(dimension_semantics=semantics, vmem_limit_bytes=vmem_mib * 2**20)


def _resident(shape):
    nd = len(shape)
    return pl.BlockSpec(shape, lambda *_: (0,) * nd, pipeline_mode=pl.Buffered(1))


def _nt_dot(a, b):
    return lax.dot_general(a, b, (((1,), (1,)), ((), ())), preferred_element_type=F32)


def _lane_lo(rows):
    return lax.broadcasted_iota(I32, (rows, 128), 1) < HEAD_DIM


def _layer_norm(z, g, b):
    mu = jnp.mean(z, axis=-1, keepdims=True)
    zc = z - mu
    var = jnp.mean(zc * zc, axis=-1, keepdims=True)
    return zc * lax.rsqrt(var + LN_EPS) * g + b


PROJ_TM = 512
PROJ_CHUNK = 512


def _proj_kernel(x_ref, w_ref, wm_ref, qkv_ref, misc_ref):
    xb = x_ref[...].astype(BF16)
    for c in range(PROJ_WIDTH // PROJ_CHUNK):
        sl = slice(c * PROJ_CHUNK, (c + 1) * PROJ_CHUNK)
        qkv_ref[:, sl] = jnp.dot(xb, w_ref[:, sl], preferred_element_type=F32).astype(BF16)
    misc_ref[...] = _nt_dot(wm_ref[...], xb)


def _project(x2d, wcat, wmisc):
    m = x2d.shape[0]
    return pl.pallas_call(
        _proj_kernel,
        grid=(m // PROJ_TM,),
        in_specs=[
            pl.BlockSpec((PROJ_TM, D_MODEL), lambda i: (i, 0)),
            _resident((D_MODEL, PROJ_WIDTH)),
            _resident((MISC_ROWS, D_MODEL)),
        ],
        out_specs=[
            pl.BlockSpec((PROJ_TM, PROJ_WIDTH), lambda i: (i, 0)),
            pl.BlockSpec((MISC_ROWS, PROJ_TM), lambda i: (0, i)),
        ],
        out_shape=[
            jax.ShapeDtypeStruct((m, PROJ_WIDTH), BF16),
            jax.ShapeDtypeStruct((MISC_ROWS, m), F32),
        ],
        compiler_params=_params(("parallel",), 48),
        name="in_proj",
    )(x2d, wcat, wmisc)


def _cum_kernel(misc_ref, bf_ref, c_ref, *, seq):
    f = misc_ref[8:16, :] + bf_ref[...]
    ls = -(jnp.maximum(-f, 0.0) + jnp.log1p(jnp.exp(-jnp.abs(f))))
    row = lax.broadcasted_iota(I32, (128, 128), 0)
    col = lax.broadcasted_iota(I32, (128, 128), 1)
    upper = jnp.where(row <= col, 1.0, 0.0).astype(BF16)
    carry = jnp.zeros((8, 1), F32)
    for j in range(seq // 128):
        blk = ls[:, j * 128:(j + 1) * 128]
        hi = blk.astype(BF16)
        r1 = blk - hi.astype(F32)
        mid = r1.astype(BF16)
        lo = (r1 - mid.astype(F32)).astype(BF16)
        cs = (jnp.dot(hi, upper, preferred_element_type=F32)
              + jnp.dot(mid, upper, preferred_element_type=F32)
              + jnp.dot(lo, upper, preferred_element_type=F32))
        c_ref[:, j * 128:(j + 1) * 128] = cs + carry
        carry = carry + cs[:, 127:128]


def _cum_forget(misc, b_forget, batch, seq):
    return pl.pallas_call(
        functools.partial(_cum_kernel, seq=seq),
        grid=(batch,),
        in_specs=[
            pl.BlockSpec((MISC_ROWS, seq), lambda b: (0, b)),
            pl.BlockSpec((8, 1), lambda b: (0, 0)),
        ],
        out_specs=pl.BlockSpec((8, seq), lambda b: (0, b)),
        out_shape=jax.ShapeDtypeStruct((8, batch * seq), F32),
        compiler_params=_params(("parallel",), 32),
        name="forget_cumsum",
    )(misc, b_forget.reshape(8, 1))


def _band_kernel(*refs, slopes, dist_scale, max_dist, kv_width, has_sinks, head_order, want_lw):
    refs = list(refs)
    sink_ref = refs.pop(0) if has_sinks else None
    q_ref, kp_ref, kc_ref, vp_ref, vc_ref = refs[:5]
    o_ref = refs[5]
    lw_ref = refs[6] if want_lw else None

    blk = pl.program_id(1)
    qi = lax.broadcasted_iota(I32, (BLOCK, 2 * BLOCK), 0)
    ki = lax.broadcasted_iota(I32, (BLOCK, 2 * BLOCK), 1)
    dist = qi - ki + BLOCK
    valid = (dist >= 0) & (dist <= max_dist) & ((blk > 0) | (ki >= BLOCK))
    distf = (dist * dist_scale).astype(F32)
    lo = _lane_lo(BLOCK)

    for p in range(4):
        q2 = q_ref[:, p * 128:(p + 1) * 128]
        ksl = slice(0, 128) if kv_width == 128 else slice(p * 128, (p + 1) * 128)
        k2 = jnp.concatenate([kp_ref[:, ksl], kc_ref[:, ksl]], axis=0)
        v2 = jnp.concatenate([vp_ref[:, ksl], vc_ref[:, ksl]], axis=0)
        outs, lws = [], []
        for half in range(2):
            h = head_order[2 * p + half]
            qm = jnp.where(lo if half == 0 else jnp.logical_not(lo), q2, jnp.zeros_like(q2))
            s = _nt_dot(qm, k2)
            s = s - slopes[h] * distf
            s = jnp.where(valid, s, NEG_INF)
            m = jnp.max(s, axis=1, keepdims=True)
            if has_sinks:
                sk = sink_ref[h]
                m = jnp.maximum(m, sk)
            pexp = jnp.exp(s - m)
            l = jnp.sum(pexp, axis=1, keepdims=True)
            if has_sinks:
                l = l + jnp.exp(sk - m)
            pv = jnp.dot(pexp.astype(BF16), v2, preferred_element_type=F32)
            outs.append(pv / l)
            lws.append(m + jnp.log(l))
        o_ref[:, p * 128:(p + 1) * 128] = jnp.where(lo, outs[0], outs[1]).astype(o_ref.dtype)
        if want_lw:
            lw_ref[:, p * 128:(p + 1) * 128] = jnp.where(
                lo, jnp.broadcast_to(lws[0], (BLOCK, 128)), jnp.broadcast_to(lws[1], (BLOCK, 128)))


def _banded_attention(arr, qcol, kcol, vcol, kv_width, *, slopes, dist_scale, max_dist,
                      sinks=None, head_order=PLAIN_ORDER, want_lw=False, out_dtype=F32, name):
    batch, length, _ = arr.shape
    nb = length // BLOCK
    kern = functools.partial(
        _band_kernel, slopes=slopes, dist_scale=dist_scale, max_dist=max_dist, kv_width=kv_width,
        has_sinks=sinks is not None, head_order=head_order, want_lw=want_lw)
    prev = lambda i: jnp.maximum(i - 1, 0)
    in_specs = [
        pl.BlockSpec((None, BLOCK, BRANCH_WIDTH), lambda b, i: (b, i, qcol)),
        pl.BlockSpec((None, BLOCK, kv_width), lambda b, i: (b, prev(i), kcol)),
        pl.BlockSpec((None, BLOCK, kv_width), lambda b, i: (b, i, kcol)),
        pl.BlockSpec((None, BLOCK, kv_width), lambda b, i: (b, prev(i), vcol)),
        pl.BlockSpec((None, BLOCK, kv_width), lambda b, i: (b, i, vcol)),
    ]
    args = [arr, arr, arr, arr, arr]
    if sinks is not None:
        in_specs = [pl.BlockSpec(memory_space=pltpu.SMEM)] + in_specs
        args = [sinks] + args
    o_spec = pl.BlockSpec((None, BLOCK, BRANCH_WIDTH), lambda b, i: (b, i, 0))
    o_shape = jax.ShapeDtypeStruct((batch, length, BRANCH_WIDTH), out_dtype)
    if want_lw:
        out_specs = [o_spec, o_spec]
        out_shape = [o_shape, jax.ShapeDtypeStruct((batch, length, BRANCH_WIDTH), F32)]
    else:
        out_specs, out_shape = o_spec, o_shape
    return pl.pallas_call(
        kern, grid=(batch, nb), in_specs=in_specs, out_specs=out_specs, out_shape=out_shape,
        compiler_params=_params(("parallel", "arbitrary"), 32), name=name,
    )(*args)


FOX_T = 256


def _fox_kernel(q_ref, k_ref, v_ref, ccol_ref, crow_ref, o_ref, m_scr, l_scr, acc_scr):
    t = FOX_T
    qi = pl.program_id(2)
    lo = _lane_lo(t)
    q2 = q_ref[...]
    qms = (jnp.where(lo, q2, jnp.zeros_like(q2)), jnp.where(lo, jnp.zeros_like(q2), q2))
    cqs = (ccol_ref[:, 0:1], ccol_ref[:, 1:2])

    m_scr[...] = jnp.full(m_scr.shape, NEG_INF, F32)
    l_scr[...] = jnp.zeros(l_scr.shape, F32)
    acc_scr[...] = jnp.zeros(acc_scr.shape, F32)

    def chunk(j, diagonal):
        start = pl.multiple_of(j * t, t)
        k2 = k_ref[pl.ds(start, t), :]
        v2 = v_ref[pl.ds(start, t), :]
        crow = crow_ref[j]
        pvs, alphas = [], []
        for half in range(2):
            s = _nt_dot(qms[half], k2)
            s = s + cqs[half] - crow[half:half + 1, :]
            if diagonal:
                r = lax.broadcasted_iota(I32, (t, t), 0)
                c = lax.broadcasted_iota(I32, (t, t), 1)
                s = jnp.where(c <= r, s, NEG_INF)
            m_prev = m_scr[half]
            m_new = jnp.maximum(m_prev, jnp.max(s, axis=1, keepdims=True))
            alpha = jnp.exp(m_prev - m_new)
            pexp = jnp.exp(s - m_new)
            l_scr[half] = alpha * l_scr[half] + jnp.sum(pexp, axis=1, keepdims=True)
            m_scr[half] = m_new
            pvs.append(jnp.dot(pexp.astype(BF16), v2, preferred_element_type=F32))
            alphas.append(alpha)
        alpha2 = jnp.where(lo, jnp.broadcast_to(alphas[0], (t, 128)), jnp.broadcast_to(alphas[1], (t, 128)))
        acc_scr[...] = alpha2 * acc_scr[...] + jnp.where(lo, pvs[0], pvs[1])

    def body(j, carry):
        chunk(j, False)
        return carry

    lax.fori_loop(0, qi, body, 0)
    chunk(qi, True)
    l2 = jnp.where(lo, jnp.broadcast_to(l_scr[0], (t, 128)), jnp.broadcast_to(l_scr[1], (t, 128)))
    o_ref[...] = (acc_scr[...] / l2).astype(o_ref.dtype)


def _fox_attention(qkv3, c_col, c_row):
    batch, seq, _ = qkv3.shape
    t = FOX_T
    nq = seq // t
    return pl.pallas_call(
        _fox_kernel,
        grid=(batch, 4, nq),
        in_specs=[
            pl.BlockSpec((None, t, 128), lambda b, p, i: (b, i, COL_DQ // 128 + p)),
            pl.BlockSpec((None, seq, 128), lambda b, p, i: (b, 0, COL_DK // 128 + p)),
            pl.BlockSpec((None, seq, 128), lambda b, p, i: (b, 0, COL_DV // 128 + p)),
            pl.BlockSpec((None, None, t, 2), lambda b, p, i: (b, p, i, 0)),
            pl.BlockSpec((None, nq, 2, t), lambda b, p, i: (p, b, 0, 0)),
        ],
        out_specs=pl.BlockSpec((None, t, 128), lambda b, p, i: (b, i, p)),
        out_shape=jax.ShapeDtypeStruct((batch, seq, BRANCH_WIDTH), BF16),
        scratch_shapes=[
            pltpu.VMEM((2, t, 1), F32),
            pltpu.VMEM((2, t, 1), F32),
            pltpu.VMEM((t, 128), F32),
        ],
        compiler_params=_params(("parallel", "parallel", "arbitrary"), 32),
        name="fox_attention",
    )(qkv3, qkv3, qkv3, c_col, c_row)


DSA_T = 256
COUNT_ROWS = 64


def _dsa_kernel(q_ref, k_ref, v_ref, iq_ref, ik_ref, iw_ref, o_ref,
                keys_scr, sel_scr, m_scr, l_scr, acc_scr, *, topk):
    t = DSA_T
    qi = pl.program_id(1)
    n_chunks = qi + 1
    q0 = qi * t
    lo = _lane_lo(t)
    qpos_row = q0 + lax.broadcasted_iota(I32, (1, t), 1)

    iq_heads = []
    for h in range(IDX_HEADS):
        iq2 = iq_ref[:, (h // 2) * 128:(h // 2 + 1) * 128]
        keep = lo if h % 2 == 0 else jnp.logical_not(lo)
        iq_heads.append(jnp.where(keep, iq2, jnp.zeros_like(iq2)))
    iws = [iw_ref[h:h + 1, :] for h in range(IDX_HEADS)]

    def score_chunk(j, carry):
        start = pl.multiple_of(j * t, t)
        ik = ik_ref[pl.ds(start, t), :]
        score = jnp.zeros((t, t), F32)
        for h in range(IDX_HEADS):
            score = score + iws[h] * jnp.maximum(_nt_dot(ik, iq_heads[h]), 0.0)
        bits = lax.bitcast_convert_type(score, I32)
        key = jnp.where(bits < 0, jnp.int32(INT_MIN) - bits, bits)
        kpos = start + lax.broadcasted_iota(I32, (t, 1), 0)
        keys_scr[pl.ds(start, t), :] = jnp.where(kpos <= qpos_row, key, jnp.int32(INT_MIN))
        return carry

    lax.fori_loop(0, n_chunks, score_chunk, 0)

    n_groups = n_chunks * (t // COUNT_ROWS)

    def count(cand, strict):
        def body(g, acc):
            blk = keys_scr[pl.ds(pl.multiple_of(g * COUNT_ROWS, COUNT_ROWS), COUNT_ROWS), :]
            hit = (blk > cand) if strict else (blk >= cand)
            return acc + jnp.where(hit, 1, 0)
        acc = lax.fori_loop(0, n_groups, body, jnp.zeros((COUNT_ROWS, t), I32))
        return jnp.sum(acc, axis=0, keepdims=True)

    def bit_round(r, thr):
        cand = thr + lax.shift_left(jnp.int32(1), 31 - r)
        return jnp.where(count(cand, False) >= topk, cand, thr)

    thr = lax.fori_loop(0, 32, bit_round, jnp.full((1, t), INT_MIN, I32))
    ties_wanted = (topk - count(thr, True)).astype(F32)

    row = lax.broadcasted_iota(I32, (t, t), 0)
    col = lax.broadcasted_iota(I32, (t, t), 1)
    lower = jnp.where(col <= row, 1.0, 0.0).astype(BF16)

    def select_chunk(j, ties_seen):
        start = pl.multiple_of(j * t, t)
        key = keys_scr[pl.ds(start, t), :]
        eq = key == thr
        eqf = jnp.where(eq, 1.0, 0.0)
        prefix = jnp.dot(lower, eqf.astype(BF16), preferred_element_type=F32)
        rank = prefix - eqf + ties_seen
        kpos = start + lax.broadcasted_iota(I32, (t, 1), 0)
        sel = ((key > thr) | (eq & (rank < ties_wanted))) & (kpos <= qpos_row)
        sel_scr[j] = jnp.where(sel, 1.0, 0.0).T
        return ties_seen + prefix[t - 1:t, :]

    lax.fori_loop(0, n_chunks, select_chunk, jnp.zeros((1, t), F32))

    qms = []
    for p in range(4):
        q2 = q_ref[:, p * 128:(p + 1) * 128]
        qms.append(jnp.where(lo, q2, jnp.zeros_like(q2)))
        qms.append(jnp.where(lo, jnp.zeros_like(q2), q2))
    m_scr[...] = jnp.full(m_scr.shape, NEG_INF, F32)
    l_scr[...] = jnp.zeros(l_scr.shape, F32)
    acc_scr[...] = jnp.zeros(acc_scr.shape, F32)
    qpos_col = q0 + lax.broadcasted_iota(I32, (t, 1), 0)

    def attend_chunk(j, carry):
        start = pl.multiple_of(j * t, t)
        k2 = k_ref[pl.ds(start, t), :]
        v2 = v_ref[pl.ds(start, t), :]
        sel = sel_scr[j] > 0.5
        distf = (qpos_col - (start + lax.broadcasted_iota(I32, (1, t), 1))).astype(F32)
        for p in range(4):
            pvs, alphas = [], []
            for half in range(2):
                h = 2 * p + half
                s = _nt_dot(qms[h], k2)
                s = s - SLOPES_C[h] * distf
                s = jnp.where(sel, s, NEG_INF)
                m_prev = m_scr[h]
                m_new = jnp.maximum(m_prev, jnp.max(s, axis=1, keepdims=True))
                alpha = jnp.exp(m_prev - m_new)
                pexp = jnp.exp(s - m_new)
                l_scr[h] = alpha * l_scr[h] + jnp.sum(pexp, axis=1, keepdims=True)
                m_scr[h] = m_new
                pvs.append(jnp.dot(pexp.astype(BF16), v2, preferred_element_type=F32))
                alphas.append(alpha)
            alpha2 = jnp.where(lo, jnp.broadcast_to(alphas[0], (t, 128)),
                               jnp.broadcast_to(alphas[1], (t, 128)))
            acc_scr[p] = alpha2 * acc_scr[p] + jnp.where(lo, pvs[0], pvs[1])
        return carry

    lax.fori_loop(0, n_chunks, attend_chunk, 0)
    for p in range(4):
        l2 = jnp.where(lo, jnp.broadcast_to(l_scr[2 * p], (t, 128)),
                       jnp.broadcast_to(l_scr[2 * p + 1], (t, 128)))
        o_ref[:, p * 128:(p + 1) * 128] = (acc_scr[p] / l2).astype(o_ref.dtype)


def _dsa_attention(qkv3, misc):
    batch, seq, _ = qkv3.shape
    t = DSA_T
    nq = seq // t
    topk = min(TOPK_MAX, seq // 4)
    return pl.pallas_call(
        functools.partial(_dsa_kernel, topk=topk),
        grid=(batch, nq),
        in_specs=[
            pl.BlockSpec((None, t, BRANCH_WIDTH), lambda b, i: (b, i, COL_CQ // 512)),
            pl.BlockSpec((None, seq, 128), lambda b, i: (b, 0, COL_CK // 128)),
            pl.BlockSpec((None, seq, 128), lambda b, i: (b, 0, COL_CV // 128)),
            pl.BlockSpec((None, t, 256), lambda b, i: (b, i, COL_CIQ // 256)),
            pl.BlockSpec((None, seq, 128), lambda b, i: (b, 0, COL_CIK // 128)),
            pl.BlockSpec((8, t), lambda b, i: (0, b * nq + i)),
        ],
        out_specs=pl.BlockSpec((None, t, BRANCH_WIDTH), lambda b, i: (b, i, 0)),
        out_shape=jax.ShapeDtypeStruct((batch, seq, BRANCH_WIDTH), BF16),
        scratch_shapes=[
            pltpu.VMEM((seq, t), I32),
            pltpu.VMEM((nq, t, t), F32),
            pltpu.VMEM((N_HEADS, t, 1), F32),
            pltpu.VMEM((N_HEADS, t, 1), F32),
            pltpu.VMEM((4, t, 128), F32),
        ],
        compiler_params=_params(("parallel", "arbitrary"), 48),
        name="dsa_attention",
    )(qkv3, qkv3, qkv3, qkv3, qkv3, misc)


MERGE_TM = 256


def _merge_kernel(x_ref, oa_ref, ob1_ref, ob4_ref, ob16_ref, lw1_ref, lw4_ref, lw16_ref,
                  oc_ref, od_ref, wg_ref, bg_ref, wb_ref, wo_ref, g_ref, b_ref, out_ref):
    x = x_ref[...]
    xb = x.astype(BF16)
    lw1, lw4, lw16 = lw1_ref[...], lw4_ref[...], lw16_ref[...]
    top = jnp.maximum(jnp.maximum(lw1, lw4), lw16)
    w1, w4, w16 = jnp.exp(lw1 - top), jnp.exp(lw4 - top), jnp.exp(lw16 - top)
    ob = (w1 * ob1_ref[...] + w4 * ob4_ref[...] + w16 * ob16_ref[...]) / (w1 + w4 + w16)
    branches = (oa_ref[...], ob.astype(BF16), oc_ref[...], od_ref[...])
    merged = jnp.zeros((MERGE_TM, D_MODEL), F32)
    for n in range(4):
        sl = slice(n * D_MODEL, (n + 1) * D_MODEL)
        proj = jnp.dot(branches[n], wb_ref[n], preferred_element_type=F32)
        gate = jax.nn.sigmoid(jnp.dot(xb, wg_ref[:, sl], preferred_element_type=F32) + bg_ref[:, sl])
        merged = merged + gate * proj
    y = jnp.dot(merged.astype(BF16), wo_ref[...], preferred_element_type=F32)
    out_ref[...] = _layer_norm(DEEPNORM_ALPHA * x + y, g_ref[...], b_ref[...])


def _merge(x2d, oa, ob_parts, lw_parts, oc, od, wg, bg, wb, wo, g, b):
    m = x2d.shape[0]
    row = lambda w: pl.BlockSpec((MERGE_TM, w), lambda i: (i, 0))
    return pl.pallas_call(
        _merge_kernel,
        grid=(m // MERGE_TM,),
        in_specs=[row(D_MODEL)] + [row(BRANCH_WIDTH)] * 9 + [
            _resident((D_MODEL, 4 * D_MODEL)),
            _resident((1, 4 * D_MODEL)),
            _resident((4, BRANCH_WIDTH, D_MODEL)),
            _resident((D_MODEL, D_MODEL)),
            _resident((1, D_MODEL)),
            _resident((1, D_MODEL)),
        ],
        out_specs=row(D_MODEL),
        out_shape=jax.ShapeDtypeStruct((m, D_MODEL), F32),
        compiler_params=_params(("parallel",), 56),
        name="branch_merge",
    )(x2d, oa, *ob_parts, *lw_parts, oc, od, wg, bg, wb, wo, g, b)


FFN_TM = 256
FFN_CHUNK = 256
HALO = 8


def _gelu_tanh(a):
    return 0.5 * a * (1.0 + jnp.tanh(np.sqrt(2.0 / np.pi).astype(np.float32) * (a + 0.044715 * (a * a * a))))


def _ffn_kernel(x_ref, xh_ref, wu_ref, wg_ref, cw_ref, cb_ref, wd_ref, g_ref, b_ref, out_ref, a_scr,
                *, tiles_per_seq):
    tm = FFN_TM
    x = x_ref[...]
    xb = x.astype(BF16)
    xhb = xh_ref[...].astype(BF16)
    seq_start = (pl.program_id(0) % tiles_per_seq) == 0
    y = jnp.zeros((tm, D_MODEL), F32)
    for c in range(D_FF // FFN_CHUNK):
        sl = slice(c * FFN_CHUNK, (c + 1) * FFN_CHUNK)
        a = jnp.dot(xb, wu_ref[:, sl], preferred_element_type=F32)
        ah = jnp.dot(xhb, wu_ref[:, sl], preferred_element_type=F32)
        a_scr[0:HALO, :] = jnp.where(seq_start, 0.0, ah)
        a_scr[HALO:HALO + tm, :] = a
        conv = cb_ref[:, sl] + (cw_ref[0:1, sl] * a_scr[HALO - 2:HALO - 2 + tm, :]
                                + cw_ref[1:2, sl] * a_scr[HALO - 1:HALO - 1 + tm, :]
                                + cw_ref[2:3, sl] * a)
        gate = jnp.dot(xb, wg_ref[:, sl], preferred_element_type=F32)
        h = (_gelu_tanh(conv) * gate).astype(BF16)
        y = y + jnp.dot(h, wd_ref[sl, :], preferred_element_type=F32)
    out_ref[...] = _layer_norm(DEEPNORM_ALPHA * x + y, g_ref[...], b_ref[...])


def _ffn(x2d, wu, wg, cw, cb, wd, g, b, seq):
    m = x2d.shape[0]
    tm = FFN_TM
    return pl.pallas_call(
        functools.partial(_ffn_kernel, tiles_per_seq=seq // tm),
        grid=(m // tm,),
        in_specs=[
            pl.BlockSpec((tm, D_MODEL), lambda i: (i, 0)),
            pl.BlockSpec((HALO, D_MODEL), lambda i: (jnp.maximum(i * (tm // HALO) - 1, 0), 0)),
            _resident((D_MODEL, D_FF)),
            _resident((D_MODEL, D_FF)),
            _resident((3, D_FF)),
            _resident((1, D_FF)),
            _resident((D_FF, D_MODEL)),
            _resident((1, D_MODEL)),
            _resident((1, D_MODEL)),
        ],
        out_specs=pl.BlockSpec((tm, D_MODEL), lambda i: (i, 0)),
        out_shape=jax.ShapeDtypeStruct((m, D_MODEL), F32),
        scratch_shapes=[pltpu.VMEM((HALO + tm, FFN_CHUNK), F32)],
        compiler_params=_params(("parallel",), 56),
        name="conv_glu_ffn",
    )(x2d, x2d, wu, wg, cw, cb, wd, g, b)


def _prep_in_proj(w):
    offs = np.cumsum((0,) + IN_SPLIT_SIZES)
    (aq, ak, av, bq, bk, bv, cq, ck, cv, ciq, cik, ciw, dq, dk, dv, df) = [
        w[:, offs[i]:offs[i + 1]] for i in range(len(IN_SPLIT_SIZES))]
    scale = HEAD_DIM ** -0.5
    aq = aq.reshape(D_MODEL, N_HEADS, HEAD_DIM)[:, jnp.array(A_ORDER), :].reshape(D_MODEL, BRANCH_WIDTH)
    pad = jnp.zeros((D_MODEL, PROJ_WIDTH - COL_CIK - 128), F32)
    wcat = jnp.concatenate(
        [aq * scale, bq * scale, bk, bv, cq * scale, dq * scale, dk, dv, ciq * scale,
         ak, av, ck, ck, cv, cv, cik, cik, pad], axis=1).astype(BF16)
    wmisc = jnp.concatenate([ciw, jnp.zeros((D_MODEL, 4), F32), df], axis=1).T.astype(BF16)
    return wcat, wmisc


def _to_classes(a, dil):
    b, l, w = a.shape
    return a.reshape(b, l // dil, dil, w).transpose(0, 2, 1, 3).reshape(b * dil, l // dil, w)


def _from_classes(a, dil, batch):
    _, lc, w = a.shape
    return a.reshape(batch, dil, lc, w).transpose(0, 2, 1, 3).reshape(batch * lc * dil, w)


def _layer(x2d, batch, seq, w_in, b_forget, sinks, w_branch, w_gate, b_gate, w_out, ln1_g, ln1_b,
           w_up, w_ffn_gate, conv_w, conv_b, w_down, ln2_g, ln2_b):
    m = batch * seq
    wcat, wmisc = _prep_in_proj(w_in)
    qkv, misc = _project(x2d, wcat, wmisc)
    qkv3 = qkv.reshape(batch, seq, PROJ_WIDTH)

    oa = _banded_attention(
        qkv3, COL_AQ // 512, COL_AK // 128, COL_AV // 128, 128, slopes=SLOPES_A, dist_scale=1,
        max_dist=A_WINDOW - 1, sinks=sinks, head_order=A_ORDER, out_dtype=BF16, name="swa_attention")

    ob_parts, lw_parts = [], []
    for window, dil in B_PATTERNS:
        if dil == 1:
            arr, cols = qkv3, (COL_BQ // 512, COL_BK // 512, COL_BV // 512)
        else:
            arr, cols = _to_classes(qkv3[:, :, COL_BQ:COL_BV + 512], dil), (0, 1, 2)
        o, lw = _banded_attention(
            arr, *cols, 512, slopes=SLOPES_B, dist_scale=dil, max_dist=window // dil, want_lw=True,
            name=f"dilated_attention_{dil}")
        ob_parts.append(_from_classes(o, dil, batch))
        lw_parts.append(_from_classes(lw, dil, batch))

    oc = _dsa_attention(qkv3, misc)

    c_t = _cum_forget(misc, b_forget, batch, seq)
    c_col = c_t.reshape(4, 2, batch, seq).transpose(2, 0, 3, 1)
    nq = seq // FOX_T
    c_row = c_t.reshape(4, 2, batch * nq, FOX_T).transpose(0, 2, 1, 3)
    od = _fox_attention(qkv3, c_col, c_row)

    wb = jnp.concatenate(
        [w_branch[0].reshape(N_HEADS, HEAD_DIM, D_MODEL)[jnp.array(A_ORDER)].reshape(1, BRANCH_WIDTH, D_MODEL),
         w_branch[1:]], axis=0).astype(BF16)
    x2d = _merge(
        x2d, oa.reshape(m, BRANCH_WIDTH), ob_parts, lw_parts, oc.reshape(m, BRANCH_WIDTH),
        od.reshape(m, BRANCH_WIDTH), w_gate.astype(BF16), b_gate.reshape(1, -1), wb, w_out.astype(BF16),
        ln1_g.reshape(1, -1), ln1_b.reshape(1, -1))
    return _ffn(x2d, w_up.astype(BF16), w_ffn_gate.astype(BF16), conv_w, conv_b.reshape(1, -1),
                w_down.astype(BF16), ln2_g.reshape(1, -1), ln2_b.reshape(1, -1), seq)


def kernel(x, w_in, b_forget, sinks, w_branch, w_gate, b_gate, w_out, ln1_g, ln1_b,
           w_up, w_ffn_gate, conv_w, conv_b, w_down, ln2_g, ln2_b):
    batch, seq, _ = x.shape
    x2d = x.reshape(batch * seq, D_MODEL)
    for l in range(w_in.shape[0]):
        x2d = _layer(x2d, batch, seq, w_in[l], b_forget[l], sinks[l], w_branch[l], w_gate[l], b_gate[l],
                     w_out[l], ln1_g[l], ln1_b[l], w_up[l], w_ffn_gate[l], conv_w[l], conv_b[l],
                     w_down[l], ln2_g[l], ln2_b[l])
    return x2d.reshape(batch, seq, D_MODEL)
```

```python
import functools

import numpy as np
import jax
import jax.numpy as jnp
from jax import lax
from jax.experimental import pallas as pl
from jax.experimental.pallas import tpu as pltpu

F32 = jnp.float32
BF16 = jnp.bfloat16
I32 = jnp.int32

D_MODEL = 1024
HEAD_DIM = 64
N_HEADS = 8
BRANCH_WIDTH = N_HEADS * HEAD_DIM
BLOCK = 128
NEG_INF = -1e30
INT_MIN = -2147483648
A_WINDOW = 128
B_PATTERNS = ((128, 1), (512, 4), (2048, 16))
IDX_HEADS = 4
TOPK_MAX = 256
D_FF = 2816
LN_EPS = 1e-5
DEPTH = 2
DEEPNORM_ALPHA = (2 * DEPTH) ** 0.25
IN_SPLIT_SIZES = (512, 128, 128, 512, 512, 512, 512, 64, 64, 256, 64, 4, 512, 512, 512, 8)

A_ORDER = (0, 4, 1, 5, 2, 6, 3, 7)
PLAIN_ORDER = tuple(range(N_HEADS))

COL_AQ, COL_BQ, COL_BK, COL_BV, COL_DK = (i * 512 for i in range(5))
COL_AK, COL_AV, COL_CK, COL_CIK = 2560, 2688, 2816, 2944
PROJ_WIDTH = 3072
MISC_ROWS = 16
ROW_DV, ROW_DQ, ROW_CQ, ROW_CIQ, ROW_CV = 0, 512, 1024, 1536, 1792
T_ROWS = 1856

CHUNK = 256


def _alibi_slopes():
    s = np.exp2(-8.0 * np.arange(1, 25, dtype=np.float32) / 24).astype(np.float32)
    return [float(v) for v in s[0::3]], [float(v) for v in s[1::3]], [float(v) for v in s[2::3]]


SLOPES_A, SLOPES_B, SLOPES_C = _alibi_slopes()


def _params(semantics, vmem_mib):
    return pltpu.CompilerParams(dimension_semantics=semantics, vmem_limit_bytes=vmem_mib * 2**20)


def _resident(shape):
    nd = len(shape)
    return pl.BlockSpec(shape, lambda *_: (0,) * nd, pipeline_mode=pl.Buffered(1))


def _nt_dot(a, b):
    return lax.dot_general(a, b, (((1,), (1,)), ((), ())), preferred_element_type=F32)


def _lane_lo(rows):
    return lax.broadcasted_iota(I32, (rows, 128), 1) < HEAD_DIM


def _split3(v):
    hi = v.astype(BF16)
    r1 = v - hi.astype(F32)
    mid = r1.astype(BF16)
    lo = (r1 - mid.astype(F32)).astype(BF16)
    return hi, mid, lo


def _layer_norm(z, g, b):
    mu = jnp.mean(z, axis=-1, keepdims=True)
    zc = z - mu
    var = jnp.mean(zc * zc, axis=-1, keepdims=True)
    return zc * lax.rsqrt(var + LN_EPS) * g + b


PROJ_TM = 512
PROJ_CHUNK = 256


def _proj_kernel(x_ref, w_ref, wm_ref, wt_ref, qkv_ref, misc_ref, t_ref):
    xb = x_ref[...].astype(BF16)
    for c in range(PROJ_WIDTH // PROJ_CHUNK):
        sl = slice(c * PROJ_CHUNK, (c + 1) * PROJ_CHUNK)
        qkv_ref[:, sl] = jnp.dot(xb, w_ref[:, sl], preferred_element_type=F32).astype(BF16)
    misc_ref[...] = _nt_dot(wm_ref[...], xb)
    for c in range(PROJ_TM // CHUNK):
        t_ref[c] = _nt_dot(wt_ref[...], xb[c * CHUNK:(c + 1) * CHUNK, :]).astype(BF16)


def _project(x2d, wcat, wmisc, wt):
    m = x2d.shape[0]
    return pl.pallas_call(
        _proj_kernel,
        grid=(m // PROJ_TM,),
        in_specs=[
            pl.BlockSpec((PROJ_TM, D_MODEL), lambda i: (i, 0)),
            _resident((D_MODEL, PROJ_WIDTH)),
            _resident((MISC_ROWS, D_MODEL)),
            _resident((T_ROWS, D_MODEL)),
        ],
        out_specs=[
            pl.BlockSpec((PROJ_TM, PROJ_WIDTH), lambda i: (i, 0)),
            pl.BlockSpec((MISC_ROWS, PROJ_TM), lambda i: (0, i)),
            pl.BlockSpec((PROJ_TM // CHUNK, T_ROWS, CHUNK), lambda i: (i, 0, 0)),
        ],
        out_shape=[
            jax.ShapeDtypeStruct((m, PROJ_WIDTH), BF16),
            jax.ShapeDtypeStruct((MISC_ROWS, m), F32),
            jax.ShapeDtypeStruct((m // CHUNK, T_ROWS, CHUNK), BF16),
        ],
        compiler_params=_params(("parallel",), 48),
        name="in_proj",
    )(x2d, wcat, wmisc, wt)


KX_ONES = 24


def _cum_kernel(misc_ref, bf_ref, c_ref, kx_ref, *, seq):
    f = misc_ref[8:16, :] + bf_ref[...]
    ls = -(jnp.maximum(-f, 0.0) + jnp.log1p(jnp.exp(-jnp.abs(f))))
    row = lax.broadcasted_iota(I32, (128, 128), 0)
    col = lax.broadcasted_iota(I32, (128, 128), 1)
    upper = jnp.where(row <= col, 1.0, 0.0).astype(BF16)
    carry = jnp.zeros((8, 1), F32)
    ones = jnp.ones((8, 128), F32)
    pad = jnp.zeros((128 - 32, 128), F32)
    for j in range(seq // 128):
        hi, mid, lo = _split3(ls[:, j * 128:(j + 1) * 128])
        cs = (jnp.dot(hi, upper, preferred_element_type=F32)
              + jnp.dot(mid, upper, preferred_element_type=F32)
              + jnp.dot(lo, upper, preferred_element_type=F32))
        c = cs + carry
        c_ref[:, j * 128:(j + 1) * 128] = c
        carry = carry + cs[:, 127:128]
        nhi, nmid, nlo = _split3(-c)
        stage = jnp.concatenate([nhi.astype(F32), nmid.astype(F32), nlo.astype(F32), ones, pad], axis=0)
        kx_ref[j * 128:(j + 1) * 128, :] = stage.T.astype(BF16)


def _cum_forget(misc, b_forget, batch, seq):
    return pl.pallas_call(
        functools.partial(_cum_kernel, seq=seq),
        grid=(batch,),
        in_specs=[
            pl.BlockSpec((MISC_ROWS, seq), lambda b: (0, b)),
            pl.BlockSpec((8, 1), lambda b: (0, 0)),
        ],
        out_specs=[
            pl.BlockSpec((8, seq), lambda b: (0, b)),
            pl.BlockSpec((seq, 128), lambda b: (b, 0)),
        ],
        out_shape=[
            jax.ShapeDtypeStruct((8, batch * seq), F32),
            jax.ShapeDtypeStruct((batch * seq, 128), BF16),
        ],
        compiler_params=_params(("parallel",), 32),
        name="forget_cumsum",
    )(misc, b_forget.reshape(8, 1))


def _band_kernel(*refs, slopes, dist_scale, max_dist, kv_width, has_sinks, head_order, want_lw):
    refs = list(refs)
    sink_ref = refs.pop(0) if has_sinks else None
    q_ref, kp_ref, kc_ref, vp_ref, vc_ref = refs[:5]
    o_ref = refs[5]
    lw_ref = refs[6] if want_lw else None

    blk = pl.program_id(1)
    qi = lax.broadcasted_iota(I32, (BLOCK, 2 * BLOCK), 0)
    ki = lax.broadcasted_iota(I32, (BLOCK, 2 * BLOCK), 1)
    dist = qi - ki + BLOCK
    valid = (dist >= 0) & (dist <= max_dist) & ((blk > 0) | (ki >= BLOCK))
    distf = (dist * dist_scale).astype(F32)
    lo = _lane_lo(BLOCK)

    for p in range(4):
        q2 = q_ref[:, p * 128:(p + 1) * 128]
        ksl = slice(0, 128) if kv_width == 128 else slice(p * 128, (p + 1) * 128)
        k2 = jnp.concatenate([kp_ref[:, ksl], kc_ref[:, ksl]], axis=0)
        v2 = jnp.concatenate([vp_ref[:, ksl], vc_ref[:, ksl]], axis=0)
        outs, lws = [], []
        for half in range(2):
            h = head_order[2 * p + half]
            qm = jnp.where(lo if half == 0 else jnp.logical_not(lo), q2, jnp.zeros_like(q2))
            s = _nt_dot(qm, k2)
            s = s - slopes[h] * distf
            s = jnp.where(valid, s, NEG_INF)
            m = jnp.max(s, axis=1, keepdims=True)
            if has_sinks:
                sk = sink_ref[h]
                m = jnp.maximum(m, sk)
            pexp = jnp.exp(s - m)
            l = jnp.sum(pexp, axis=1, keepdims=True)
            if has_sinks:
                l = l + jnp.exp(sk - m)
            pv = jnp.dot(pexp.astype(BF16), v2, preferred_element_type=F32)
            outs.append(pv / l)
            lws.append(m + jnp.log(l))
        o_ref[:, p * 128:(p + 1) * 128] = jnp.where(lo, outs[0], outs[1]).astype(o_ref.dtype)
        if want_lw:
            lw_ref[:, p * 128:(p + 1) * 128] = jnp.where(
                lo, jnp.broadcast_to(lws[0], (BLOCK, 128)), jnp.broadcast_to(lws[1], (BLOCK, 128)))


def _banded_attention(arr, qcol, kcol, vcol, kv_width, *, slopes, dist_scale, max_dist,
                      sinks=None, head_order=PLAIN_ORDER, want_lw=False, out_dtype=F32, name):
    batch, length, _ = arr.shape
    nb = length // BLOCK
    kern = functools.partial(
        _band_kernel, slopes=slopes, dist_scale=dist_scale, max_dist=max_dist, kv_width=kv_width,
        has_sinks=sinks is not None, head_order=head_order, want_lw=want_lw)
    prev = lambda i: jnp.maximum(i - 1, 0)
    in_specs = [
        pl.BlockSpec((None, BLOCK, BRANCH_WIDTH), lambda b, i: (b, i, qcol)),
        pl.BlockSpec((None, BLOCK, kv_width), lambda b, i: (b, prev(i), kcol)),
        pl.BlockSpec((None, BLOCK, kv_width), lambda b, i: (b, i, kcol)),
        pl.BlockSpec((None, BLOCK, kv_width), lambda b, i: (b, prev(i), vcol)),
        pl.BlockSpec((None, BLOCK, kv_width), lambda b, i: (b, i, vcol)),
    ]
    args = [arr, arr, arr, arr, arr]
    if sinks is not None:
        in_specs = [pl.BlockSpec(memory_space=pltpu.SMEM)] + in_specs
        args = [sinks] + args
    o_spec = pl.BlockSpec((None, BLOCK, BRANCH_WIDTH), lambda b, i: (b, i, 0))
    o_shape = jax.ShapeDtypeStruct((batch, length, BRANCH_WIDTH), out_dtype)
    if want_lw:
        out_specs = [o_spec, o_spec]
        out_shape = [o_shape, jax.ShapeDtypeStruct((batch, length, BRANCH_WIDTH), F32)]
    else:
        out_specs, out_shape = o_spec, o_shape
    return pl.pallas_call(
        kern, grid=(batch, nb), in_specs=in_specs, out_specs=out_specs, out_shape=out_shape,
        compiler_params=_params(("parallel", "arbitrary"), 32), name=name,
    )(*args)


def _stash_scores(s, slot, h, s_scr, mc_scr):
    s_scr[slot, h] = s
    mc_scr[slot, h] = jnp.max(s, axis=0, keepdims=True)


def _fold_scores(slot, h, vt, s_scr, mc_scr, m_scr, l_scr, acc_scr):
    s = s_scr[slot, h]
    m_prev = m_scr[h]
    m_new = jnp.maximum(m_prev, mc_scr[slot, h])
    alpha = jnp.exp(m_prev - m_new)
    pexp = jnp.exp(s - m_new)
    l_scr[h] = alpha * l_scr[h] + jnp.sum(pexp, axis=0, keepdims=True)
    m_scr[h] = m_new
    rows = slice(h * HEAD_DIM, (h + 1) * HEAD_DIM)
    acc_scr[rows, :] = alpha * acc_scr[rows, :] + jnp.dot(vt, pexp.astype(BF16), preferred_element_type=F32)


def _init_softmax(m_scr, l_scr, acc_scr):
    m_scr[...] = jnp.full(m_scr.shape, NEG_INF, F32)
    l_scr[...] = jnp.zeros(l_scr.shape, F32)
    acc_scr[...] = jnp.zeros(acc_scr.shape, F32)


def _write_heads(o_ref, l_scr, acc_scr):
    t = CHUNK
    for p in range(4):
        l2 = jnp.concatenate(
            [jnp.broadcast_to(l_scr[2 * p], (HEAD_DIM, t)),
             jnp.broadcast_to(l_scr[2 * p + 1], (HEAD_DIM, t))], axis=0)
        o_ref[:, p * 128:(p + 1) * 128] = (acc_scr[p * 128:(p + 1) * 128, :] / l2).T.astype(o_ref.dtype)


def _row_select(rows, width, pieces):
    ridx = lax.broadcasted_iota(I32, (rows, width), 0)
    out = jnp.zeros((rows, width), F32)
    for r, piece in pieces.items():
        out = jnp.where(ridx == r, piece, out)
    return out


def _fox_kernel(k_ref, vt_ref, qt_ref, kx_ref, crow_ref, o_ref,
                w_scr, s_scr, mc_scr, m_scr, l_scr, acc_scr):
    t = CHUNK
    qi = pl.program_id(1)
    lane = lax.broadcasted_iota(I32, (t, 128), 1)
    lo = lane < HEAD_DIM

    c_terms = [x.astype(F32) for x in _split3(crow_ref[...])]
    ridx = lax.broadcasted_iota(I32, (128, t), 0)
    for p in range(4):
        pieces = {}
        for i in range(3):
            pieces[KX_ONES + i] = c_terms[i][2 * p:2 * p + 1, :]
            pieces[KX_ONES + 3 + i] = c_terms[i][2 * p + 1:2 * p + 2, :]
        ext = jnp.where(ridx < KX_ONES, 1.0, _row_select(128, t, pieces))
        w_scr[p, 0:128, :] = qt_ref[p * 128:(p + 1) * 128, :]
        w_scr[p, 128:256, :] = ext.astype(BF16)

    def bias_lanes(h, half):
        first = KX_ONES + 3 * half
        return (lane == h) | (lane == 8 + h) | (lane == 16 + h) | ((lane >= first) & (lane < first + 3))

    _init_softmax(m_scr, l_scr, acc_scr)
    causal = lax.broadcasted_iota(I32, (t, t), 0) <= lax.broadcasted_iota(I32, (t, t), 1)

    def score(j, diagonal):
        start = pl.multiple_of(j * t, t)
        kx = kx_ref[pl.ds(start, t), :]
        zeros = jnp.zeros_like(kx)
        for p in range(4):
            k2 = k_ref[pl.ds(start, t), p * 128:(p + 1) * 128]
            lhs_a = jnp.concatenate(
                [jnp.where(lo, k2, zeros), jnp.where(bias_lanes(2 * p, 0), kx, zeros)], axis=1)
            lhs_b = jnp.concatenate(
                [jnp.where(lo, zeros, k2), jnp.where(bias_lanes(2 * p + 1, 1), kx, zeros)], axis=1)
            s2 = jnp.dot(jnp.concatenate([lhs_a, lhs_b], axis=0), w_scr[p], preferred_element_type=F32)
            for half in range(2):
                s = s2[half * t:(half + 1) * t, :]
                if diagonal:
                    s = jnp.where(causal, s, NEG_INF)
                _stash_scores(s, j % 2, 2 * p + half, s_scr, mc_scr)

    def fold(j):
        vt = vt_ref[j]
        for h in range(N_HEADS):
            _fold_scores(j % 2, h, vt[h * HEAD_DIM:(h + 1) * HEAD_DIM, :], s_scr, mc_scr, m_scr, l_scr, acc_scr)

    @pl.when(qi > 0)
    def _():
        score(0, False)

    def body(j, carry):
        score(j + 1, False)
        fold(j)
        return carry

    lax.fori_loop(0, jnp.maximum(qi - 1, 0), body, 0)
    score(qi, True)

    @pl.when(qi > 0)
    def _():
        fold(qi - 1)

    fold(qi)
    _write_heads(o_ref, l_scr, acc_scr)


def _fox_attention(qkv3, tproj, kx, c_t):
    batch, seq, _ = qkv3.shape
    t = CHUNK
    nq = seq // t
    return pl.pallas_call(
        _fox_kernel,
        grid=(batch, nq),
        in_specs=[
            pl.BlockSpec((None, seq, BRANCH_WIDTH), lambda b, i: (b, 0, COL_DK // 512)),
            pl.BlockSpec((nq, BRANCH_WIDTH, t), lambda b, i: (b, ROW_DV // 512, 0)),
            pl.BlockSpec((None, BRANCH_WIDTH, t), lambda b, i: (b * nq + i, ROW_DQ // 512, 0)),
            pl.BlockSpec((seq, 128), lambda b, i: (b, 0)),
            pl.BlockSpec((8, t), lambda b, i: (0, b * nq + i)),
        ],
        out_specs=pl.BlockSpec((None, t, BRANCH_WIDTH), lambda b, i: (b, i, 0)),
        out_shape=jax.ShapeDtypeStruct((batch, seq, BRANCH_WIDTH), BF16),
        scratch_shapes=[
            pltpu.VMEM((4, 256, t), BF16),
            pltpu.VMEM((2, N_HEADS, t, t), F32),
            pltpu.VMEM((2, N_HEADS, 1, t), F32),
            pltpu.VMEM((N_HEADS, 1, t), F32),
            pltpu.VMEM((N_HEADS, 1, t), F32),
            pltpu.VMEM((BRANCH_WIDTH, t), F32),
        ],
        compiler_params=_params(("parallel", "arbitrary"), 48),
        name="fox_attention",
    )(qkv3, tproj, tproj, kx, c_t)


COUNT_ROWS = 128


def _bf16_terms(value):
    out, rest = [], np.float32(value)
    for _ in range(3):
        bits = np.array([rest], np.float32).view(np.uint32)
        rounded = ((bits + 0x7FFF + ((bits >> 16) & 1)) & 0xFFFF0000).astype(np.uint32)
        term = rounded.view(np.float32)[0]
        out.append(float(term))
        rest = np.float32(rest - term)
    return out


SLOPE_TERMS_C = [_bf16_terms(s) for s in SLOPES_C]


def _dsa_kernel(k_ref, vt_ref, qt_ref, iqt_ref, ik_ref, iw_ref, o_ref,
                keys_scr, w_scr, s_scr, mc_scr, m_scr, l_scr, acc_scr, *, topk):
    t = CHUNK
    qi = pl.program_id(1)
    n_chunks = qi + 1
    q0 = qi * t
    lane = lax.broadcasted_iota(I32, (t, 128), 1)
    lo = lane < HEAD_DIM
    qpos_row = q0 + lax.broadcasted_iota(I32, (1, t), 1)

    iws = [iw_ref[h:h + 1, :] for h in range(IDX_HEADS)]

    def score_chunk(j, carry):
        start = pl.multiple_of(j * t, t)
        ik = ik_ref[pl.ds(start, t), :]
        zeros = jnp.zeros_like(ik)
        ik_lo, ik_hi = jnp.where(lo, ik, zeros), jnp.where(lo, zeros, ik)
        lhs = jnp.concatenate(
            [jnp.concatenate([ik_lo, zeros], axis=1), jnp.concatenate([ik_hi, zeros], axis=1),
             jnp.concatenate([zeros, ik_lo], axis=1), jnp.concatenate([zeros, ik_hi], axis=1)], axis=0)
        dots = jnp.dot(lhs, iqt_ref[...], preferred_element_type=F32)
        score = jnp.zeros((t, t), F32)
        for h in range(IDX_HEADS):
            score = score + iws[h] * jnp.maximum(dots[h * t:(h + 1) * t, :], 0.0)
        bits = lax.bitcast_convert_type(score, I32)
        key = jnp.where(bits < 0, jnp.int32(INT_MIN) - bits, bits)
        kpos = start + lax.broadcasted_iota(I32, (t, 1), 0)
        keys_scr[pl.ds(start, t), :] = jnp.where(kpos <= qpos_row, key, jnp.int32(INT_MIN))
        return carry

    lax.fori_loop(0, n_chunks, score_chunk, 0)

    n_groups = n_chunks * (t // COUNT_ROWS)

    def count(cand, strict):
        def body(g, acc):
            blk = keys_scr[pl.ds(pl.multiple_of(g * COUNT_ROWS, COUNT_ROWS), COUNT_ROWS), :]
            hit = (blk > cand) if strict else (blk >= cand)
            ones = jnp.where(hit, 1, 0).reshape(COUNT_ROWS // 32, 32, t)
            return acc + jnp.sum(ones, axis=0)
        acc = lax.fori_loop(0, n_groups, body, jnp.zeros((32, t), I32))
        return jnp.sum(acc, axis=0, keepdims=True)

    def bit_round(r, thr):
        cand = thr + lax.shift_left(jnp.int32(1), 31 - r)
        return jnp.where(count(cand, False) >= topk, cand, thr)

    thr = lax.fori_loop(0, 32, bit_round, jnp.full((1, t), INT_MIN, I32))
    ties_wanted = jnp.where(thr == INT_MIN, 0, topk - count(thr, True)).astype(F32)

    tpos = qpos_row.astype(F32)
    for h in range(N_HEADS):
        p, half = divmod(h, 2)
        neg_t = _split3(-(np.float32(SLOPES_C[h]) * tpos))
        pieces = {}
        for i in range(3):
            pieces[i] = np.float32(64.0 * SLOPE_TERMS_C[h][i])
            pieces[3 + i] = np.float32(SLOPE_TERMS_C[h][i])
            pieces[6 + i] = neg_t[i].astype(F32)
        base = half * 128
        w_scr[p, base:base + HEAD_DIM, :] = qt_ref[h * HEAD_DIM:(h + 1) * HEAD_DIM, :]
        w_scr[p, base + HEAD_DIM:base + 128, :] = _row_select(HEAD_DIM, t, pieces).astype(BF16)

    krow = lax.broadcasted_iota(I32, (t, 128), 0)
    rel = lane - HEAD_DIM

    def key_extras(start):
        a = lax.shift_right_logical(start + krow, 6).astype(F32)
        b = (krow & 63).astype(F32)
        ex = jnp.where((rel >= 0) & (rel < 3), a, 0.0)
        ex = jnp.where((rel >= 3) & (rel < 6), b, ex)
        ex = jnp.where((rel >= 6) & (rel < 9), 1.0, ex)
        return ex.astype(BF16)

    _init_softmax(m_scr, l_scr, acc_scr)
    row = lax.broadcasted_iota(I32, (t, t), 0)
    col = lax.broadcasted_iota(I32, (t, t), 1)
    lower = jnp.where(col <= row, 1.0, 0.0).astype(BF16)

    def score(j, ties_seen):
        start = pl.multiple_of(j * t, t)
        key = keys_scr[pl.ds(start, t), :]
        eq = key == thr
        eqf = jnp.where(eq, 1.0, 0.0)
        prefix = jnp.dot(lower, eqf.astype(BF16), preferred_element_type=F32)
        rank = prefix - eqf + ties_seen
        sel = (key > thr) | (eq & (rank < ties_wanted))
        k_aug = jnp.where(lo, k_ref[pl.ds(start, t), :], key_extras(start))
        zeros = jnp.zeros_like(k_aug)
        lhs = jnp.concatenate(
            [jnp.concatenate([k_aug, zeros], axis=1), jnp.concatenate([zeros, k_aug], axis=1)], axis=0)
        for p in range(4):
            s2 = jnp.dot(lhs, w_scr[p], preferred_element_type=F32)
            for half in range(2):
                s = jnp.where(sel, s2[half * t:(half + 1) * t, :], NEG_INF)
                _stash_scores(s, j % 2, 2 * p + half, s_scr, mc_scr)
        return ties_seen + prefix[t - 1:t, :]

    def fold(j):
        vt = vt_ref[j]
        for h in range(N_HEADS):
            _fold_scores(j % 2, h, vt, s_scr, mc_scr, m_scr, l_scr, acc_scr)

    def body(j, ties_seen):
        ties_seen = score(j + 1, ties_seen)
        fold(j)
        return ties_seen

    lax.fori_loop(0, n_chunks - 1, body, score(0, jnp.zeros((1, t), F32)))
    fold(n_chunks - 1)
    _write_heads(o_ref, l_scr, acc_scr)


def _dsa_attention(qkv3, tproj, misc):
    batch, seq, _ = qkv3.shape
    t = CHUNK
    nq = seq // t
    topk = min(TOPK_MAX, seq // 4)
    return pl.pallas_call(
        functools.partial(_dsa_kernel, topk=topk),
        grid=(batch, nq),
        in_specs=[
            pl.BlockSpec((None, seq, 128), lambda b, i: (b, 0, COL_CK // 128)),
            pl.BlockSpec((nq, HEAD_DIM, t), lambda b, i: (b, ROW_CV // HEAD_DIM, 0)),
            pl.BlockSpec((None, BRANCH_WIDTH, t), lambda b, i: (b * nq + i, ROW_CQ // 512, 0)),
            pl.BlockSpec((None, 256, t), lambda b, i: (b * nq + i, ROW_CIQ // 256, 0)),
            pl.BlockSpec((None, seq, 128), lambda b, i: (b, 0, COL_CIK // 128)),
            pl.BlockSpec((8, t), lambda b, i: (0, b * nq + i)),
        ],
        out_specs=pl.BlockSpec((None, t, BRANCH_WIDTH), lambda b, i: (b, i, 0)),
        out_shape=jax.ShapeDtypeStruct((batch, seq, BRANCH_WIDTH), BF16),
        scratch_shapes=[
            pltpu.VMEM((seq, t), I32),
            pltpu.VMEM((4, 256, t), BF16),
            pltpu.VMEM((2, N_HEADS, t, t), F32),
            pltpu.VMEM((2, N_HEADS, 1, t), F32),
            pltpu.VMEM((N_HEADS, 1, t), F32),
            pltpu.VMEM((N_HEADS, 1, t), F32),
            pltpu.VMEM((BRANCH_WIDTH, t), F32),
        ],
        compiler_params=_params(("parallel", "arbitrary"), 48),
        name="dsa_attention",
    )(qkv3, tproj, tproj, tproj, qkv3, misc)


MERGE_TM = 256


def _merge_kernel(x_ref, oa_ref, ob1_ref, ob4_ref, ob16_ref, lw1_ref, lw4_ref, lw16_ref,
                  oc_ref, od_ref, wg_ref, bg_ref, wb_ref, wo_ref, g_ref, b_ref, out_ref):
    x = x_ref[...]
    xb = x.astype(BF16)
    lw1, lw4, lw16 = lw1_ref[...], lw4_ref[...], lw16_ref[...]
    top = jnp.maximum(jnp.maximum(lw1, lw4), lw16)
    w1, w4, w16 = jnp.exp(lw1 - top), jnp.exp(lw4 - top), jnp.exp(lw16 - top)
    ob = (w1 * ob1_ref[...] + w4 * ob4_ref[...] + w16 * ob16_ref[...]) / (w1 + w4 + w16)
    branches = (oa_ref[...], ob.astype(BF16), oc_ref[...], od_ref[...])
    merged = jnp.zeros((MERGE_TM, D_MODEL), F32)
    for n in range(4):
        sl = slice(n * D_MODEL, (n + 1) * D_MODEL)
        proj = jnp.dot(branches[n], wb_ref[n], preferred_element_type=F32)
        gate = jax.nn.sigmoid(jnp.dot(xb, wg_ref[:, sl], preferred_element_type=F32) + bg_ref[:, sl])
        merged = merged + gate * proj
    y = jnp.dot(merged.astype(BF16), wo_ref[...], preferred_element_type=F32)
    out_ref[...] = _layer_norm(DEEPNORM_ALPHA * x + y, g_ref[...], b_ref[...])


def _merge(x2d, oa, ob_parts, lw_parts, oc, od, wg, bg, wb, wo, g, b):
    m = x2d.shape[0]
    row = lambda w: pl.BlockSpec((MERGE_TM, w), lambda i: (i, 0))
    return pl.pallas_call(
        _merge_kernel,
        grid=(m // MERGE_TM,),
        in_specs=[row(D_MODEL)] + [row(BRANCH_WIDTH)] * 9 + [
            _resident((D_MODEL, 4 * D_MODEL)),
            _resident((1, 4 * D_MODEL)),
            _resident((4, BRANCH_WIDTH, D_MODEL)),
            _resident((D_MODEL, D_MODEL)),
            _resident((1, D_MODEL)),
            _resident((1, D_MODEL)),
        ],
        out_specs=row(D_MODEL),
        out_shape=jax.ShapeDtypeStruct((m, D_MODEL), F32),
        compiler_params=_params(("parallel",), 56),
        name="branch_merge",
    )(x2d, oa, *ob_parts, *lw_parts, oc, od, wg, bg, wb, wo, g, b)


FFN_TM = 256
FFN_CHUNK = 256
HALO = 8


def _gelu_tanh(a):
    return 0.5 * a * (1.0 + jnp.tanh(np.float32(np.sqrt(2.0 / np.pi)) * (a + 0.044715 * (a * a * a))))


def _ffn_kernel(x_ref, xh_ref, wu_ref, wg_ref, cw_ref, cb_ref, wd_ref, g_ref, b_ref, out_ref, a_scr,
                *, tiles_per_seq):
    tm = FFN_TM
    x = x_ref[...]
    xb = x.astype(BF16)
    xhb = xh_ref[...].astype(BF16)
    seq_start = (pl.program_id(0) % tiles_per_seq) == 0
    y = jnp.zeros((tm, D_MODEL), F32)
    for c in range(D_FF // FFN_CHUNK):
        sl = slice(c * FFN_CHUNK, (c + 1) * FFN_CHUNK)
        a = jnp.dot(xb, wu_ref[:, sl], preferred_element_type=F32)
        ah = jnp.dot(xhb, wu_ref[:, sl], preferred_element_type=F32)
        a_scr[0:HALO, :] = jnp.where(seq_start, 0.0, ah)
        a_scr[HALO:HALO + tm, :] = a
        conv = cb_ref[:, sl] + (cw_ref[0:1, sl] * a_scr[HALO - 2:HALO - 2 + tm, :]
                                + cw_ref[1:2, sl] * a_scr[HALO - 1:HALO - 1 + tm, :]
                                + cw_ref[2:3, sl] * a)
        gate = jnp.dot(xb, wg_ref[:, sl], preferred_element_type=F32)
        h = (_gelu_tanh(conv) * gate).astype(BF16)
        y = y + jnp.dot(h, wd_ref[sl, :], preferred_element_type=F32)
    out_ref[...] = _layer_norm(DEEPNORM_ALPHA * x + y, g_ref[...], b_ref[...])


def _ffn(x2d, wu, wg, cw, cb, wd, g, b, seq):
    m = x2d.shape[0]
    tm = FFN_TM
    return pl.pallas_call(
        functools.partial(_ffn_kernel, tiles_per_seq=seq // tm),
        grid=(m // tm,),
        in_specs=[
            pl.BlockSpec((tm, D_MODEL), lambda i: (i, 0)),
            pl.BlockSpec((HALO, D_MODEL), lambda i: (jnp.maximum(i * (tm // HALO) - 1, 0), 0)),
            _resident((D_MODEL, D_FF)),
            _resident((D_MODEL, D_FF)),
            _resident((3, D_FF)),
            _resident((1, D_FF)),
            _resident((D_FF, D_MODEL)),
            _resident((1, D_MODEL)),
            _resident((1, D_MODEL)),
        ],
        out_specs=pl.BlockSpec((tm, D_MODEL), lambda i: (i, 0)),
        out_shape=jax.ShapeDtypeStruct((m, D_MODEL), F32),
        scratch_shapes=[pltpu.VMEM((HALO + tm, FFN_CHUNK), F32)],
        compiler_params=_params(("parallel",), 56),
        name="conv_glu_ffn",
    )(x2d, x2d, wu, wg, cw, cb, wd, g, b)


def _prep_in_proj(w):
    offs = np.cumsum((0,) + IN_SPLIT_SIZES)
    (aq, ak, av, bq, bk, bv, cq, ck, cv, ciq, cik, ciw, dq, dk, dv, df) = [
        w[:, offs[i]:offs[i + 1]] for i in range(len(IN_SPLIT_SIZES))]
    scale = HEAD_DIM ** -0.5
    aq = aq.reshape(D_MODEL, N_HEADS, HEAD_DIM)[:, jnp.array(A_ORDER), :].reshape(D_MODEL, BRANCH_WIDTH)
    pad64 = jnp.zeros((D_MODEL, HEAD_DIM), F32)
    wcat = jnp.concatenate([aq * scale, bq * scale, bk, bv, dk, ak, av, ck, pad64, cik, cik], axis=1).astype(BF16)
    wmisc = jnp.concatenate([ciw, jnp.zeros((D_MODEL, 4), F32), df], axis=1).T.astype(BF16)
    wt = jnp.concatenate([dv, dq * scale, cq * scale, ciq * scale, cv], axis=1).T.astype(BF16)
    return wcat, wmisc, wt


def _to_classes(a, dil):
    b, l, w = a.shape
    return a.reshape(b, l // dil, dil, w).transpose(0, 2, 1, 3).reshape(b * dil, l // dil, w)


def _from_classes(a, dil, batch):
    _, lc, w = a.shape
    return a.reshape(batch, dil, lc, w).transpose(0, 2, 1, 3).reshape(batch * lc * dil, w)


def _layer(x2d, batch, seq, w_in, b_forget, sinks, w_branch, w_gate, b_gate, w_out, ln1_g, ln1_b,
           w_up, w_ffn_gate, conv_w, conv_b, w_down, ln2_g, ln2_b):
    m = batch * seq
    wcat, wmisc, wt = _prep_in_proj(w_in)
    qkv, misc, tproj = _project(x2d, wcat, wmisc, wt)
    qkv3 = qkv.reshape(batch, seq, PROJ_WIDTH)

    oa = _banded_attention(
        qkv3, COL_AQ // 512, COL_AK // 128, COL_AV // 128, 128, slopes=SLOPES_A, dist_scale=1,
        max_dist=A_WINDOW - 1, sinks=sinks, head_order=A_ORDER, out_dtype=BF16, name="swa_attention")

    ob_parts, lw_parts = [], []
    for window, dil in B_PATTERNS:
        if dil == 1:
            arr, cols = qkv3, (COL_BQ // 512, COL_BK // 512, COL_BV // 512)
        else:
            arr, cols = _to_classes(qkv3[:, :, COL_BQ:COL_BV + 512], dil), (0, 1, 2)
        o, lw = _banded_attention(
            arr, *cols, 512, slopes=SLOPES_B, dist_scale=dil, max_dist=window // dil, want_lw=True,
            name=f"dilated_attention_{dil}")
        ob_parts.append(_from_classes(o, dil, batch))
        lw_parts.append(_from_classes(lw, dil, batch))

    oc = _dsa_attention(qkv3, tproj, misc)

    c_t, kx = _cum_forget(misc, b_forget, batch, seq)
    od = _fox_attention(qkv3, tproj, kx, c_t)

    wb = jnp.concatenate(
        [w_branch[0].reshape(N_HEADS, HEAD_DIM, D_MODEL)[jnp.array(A_ORDER)].reshape(1, BRANCH_WIDTH, D_MODEL),
         w_branch[1:]], axis=0).astype(BF16)
    x2d = _merge(
        x2d, oa.reshape(m, BRANCH_WIDTH), ob_parts, lw_parts, oc.reshape(m, BRANCH_WIDTH),
        od.reshape(m, BRANCH_WIDTH), w_gate.astype(BF16), b_gate.reshape(1, -1), wb, w_out.astype(BF16),
        ln1_g.reshape(1, -1), ln1_b.reshape(1, -1))
    return _ffn(x2d, w_up.astype(BF16), w_ffn_gate.astype(BF16), conv_w, conv_b.reshape(1, -1),
                w_down.astype(BF16), ln2_g.reshape(1, -1), ln2_b.reshape(1, -1), seq)


def kernel(x, w_in, b_forget, sinks, w_branch, w_gate, b_gate, w_out, ln1_g, ln1_b,
           w_up, w_ffn_gate, conv_w, conv_b, w_down, ln2_g, ln2_b):
    batch, seq, _ = x.shape
    x2d = x.reshape(batch * seq, D_MODEL)
    for l in range(w_in.shape[0]):
        x2d = _layer(x2d, batch, seq, w_in[l], b_forget[l], sinks[l], w_branch[l], w_gate[l], b_gate[l],
                     w_out[l], ln1_g[l], ln1_b[l], w_up[l], w_ffn_gate[l], conv_w[l], conv_b[l],
                     w_down[l], ln2_g[l], ln2_b[l])
    return x2d.reshape(batch, seq, D_MODEL)
```

```python
import functools

import numpy as np
import jax
import jax.numpy as jnp
from jax import lax
from jax.experimental import pallas as pl
from jax.experimental.pallas import tpu as pltpu

F32 = jnp.float32
BF16 = jnp.bfloat16
I32 = jnp.int32

D_MODEL = 1024
HEAD_DIM = 64
N_HEADS = 8
BRANCH_WIDTH = N_HEADS * HEAD_DIM
BLOCK = 128
NEG_INF = -1e30
INT_MIN = -2147483648
LOG2E = np.float32(1.4426950408889634)
A_WINDOW = 128
B_PATTERNS = ((128, 1), (512, 4), (2048, 16))
IDX_HEADS = 4
TOPK_MAX = 256
D_FF = 2816
LN_EPS = 1e-5
DEPTH = 2
DEEPNORM_ALPHA = (2 * DEPTH) ** 0.25
IN_SPLIT_SIZES = (512, 128, 128, 512, 512, 512, 512, 64, 64, 256, 64, 4, 512, 512, 512, 8)

A_ORDER = (0, 4, 1, 5, 2, 6, 3, 7)
PLAIN_ORDER = tuple(range(N_HEADS))

COL_AQ, COL_BQ, COL_BK, COL_BV, COL_DK = (i * 512 for i in range(5))
COL_AK, COL_AV, COL_CK, COL_CIK = 2560, 2688, 2816, 2944
PROJ_WIDTH = 3072
MISC_ROWS = 16
ROW_DV, ROW_DQ, ROW_CQ, ROW_CIQ, ROW_CV = 0, 512, 1024, 1536, 1792
T_ROWS = 1856

CHUNK = 256


def _alibi_slopes():
    s = np.exp2(-8.0 * np.arange(1, 25, dtype=np.float32) / 24).astype(np.float32)
    return [float(v) for v in s[0::3]], [float(v) for v in s[1::3]], [float(v) for v in s[2::3]]


SLOPES_A, SLOPES_B, SLOPES_C = _alibi_slopes()


def _params(semantics, vmem_mib):
    return pltpu.CompilerParams(dimension_semantics=semantics, vmem_limit_bytes=vmem_mib * 2**20)


def _resident(shape):
    nd = len(shape)
    return pl.BlockSpec(shape, lambda *_: (0,) * nd, pipeline_mode=pl.Buffered(1))


def _nt_dot(a, b):
    return lax.dot_general(a, b, (((1,), (1,)), ((), ())), preferred_element_type=F32)


def _lane_lo(rows):
    return lax.broadcasted_iota(I32, (rows, 128), 1) < HEAD_DIM


def _split3(v):
    hi = v.astype(BF16)
    r1 = v - hi.astype(F32)
    mid = r1.astype(BF16)
    lo = (r1 - mid.astype(F32)).astype(BF16)
    return hi, mid, lo


def _layer_norm(z, g, b):
    mu = jnp.mean(z, axis=-1, keepdims=True)
    zc = z - mu
    var = jnp.mean(zc * zc, axis=-1, keepdims=True)
    return zc * lax.rsqrt(var + LN_EPS) * g + b


PROJ_TM = 512
PROJ_CHUNK = 256


def _proj_kernel(x_ref, w_ref, wm_ref, wt_ref, ts_ref, qkv_ref, misc_ref, t_ref):
    xb = x_ref[...].astype(BF16)
    for c in range(PROJ_WIDTH // PROJ_CHUNK):
        sl = slice(c * PROJ_CHUNK, (c + 1) * PROJ_CHUNK)
        qkv_ref[:, sl] = jnp.dot(xb, w_ref[:, sl], preferred_element_type=F32).astype(BF16)
    misc_ref[...] = _nt_dot(wm_ref[...], xb)
    for c in range(PROJ_TM // CHUNK):
        acc = _nt_dot(wt_ref[...], xb[c * CHUNK:(c + 1) * CHUNK, :])
        t_ref[c] = (acc * ts_ref[...]).astype(BF16)


def _project(x2d, wcat, wmisc, wt, tscale):
    m = x2d.shape[0]
    return pl.pallas_call(
        _proj_kernel,
        grid=(m // PROJ_TM,),
        in_specs=[
            pl.BlockSpec((PROJ_TM, D_MODEL), lambda i: (i, 0)),
            _resident((D_MODEL, PROJ_WIDTH)),
            _resident((MISC_ROWS, D_MODEL)),
            _resident((T_ROWS, D_MODEL)),
            _resident((T_ROWS, 1)),
        ],
        out_specs=[
            pl.BlockSpec((PROJ_TM, PROJ_WIDTH), lambda i: (i, 0)),
            pl.BlockSpec((MISC_ROWS, PROJ_TM), lambda i: (0, i)),
            pl.BlockSpec((PROJ_TM // CHUNK, T_ROWS, CHUNK), lambda i: (i, 0, 0)),
        ],
        out_shape=[
            jax.ShapeDtypeStruct((m, PROJ_WIDTH), BF16),
            jax.ShapeDtypeStruct((MISC_ROWS, m), F32),
            jax.ShapeDtypeStruct((m // CHUNK, T_ROWS, CHUNK), BF16),
        ],
        compiler_params=_params(("parallel",), 48),
        name="in_proj",
    )(x2d, wcat, wmisc, wt, tscale)


KX_ONES = 24


def _cum_kernel(misc_ref, bf_ref, c_ref, kx_ref, *, seq):
    f = misc_ref[8:16, :] + bf_ref[...]
    ls = -(jnp.maximum(-f, 0.0) + jnp.log1p(jnp.exp(-jnp.abs(f))))
    row = lax.broadcasted_iota(I32, (128, 128), 0)
    col = lax.broadcasted_iota(I32, (128, 128), 1)
    upper = jnp.where(row <= col, 1.0, 0.0).astype(BF16)
    carry = jnp.zeros((8, 1), F32)
    ones = jnp.ones((8, 128), F32)
    pad = jnp.zeros((128 - 32, 128), F32)
    for j in range(seq // 128):
        hi, mid, lo = _split3(ls[:, j * 128:(j + 1) * 128])
        cs = (jnp.dot(hi, upper, preferred_element_type=F32)
              + jnp.dot(mid, upper, preferred_element_type=F32)
              + jnp.dot(lo, upper, preferred_element_type=F32))
        c = cs + carry
        c_ref[:, j * 128:(j + 1) * 128] = c
        carry = carry + cs[:, 127:128]
        nhi, nmid, nlo = _split3(-(c * LOG2E))
        stage = jnp.concatenate([nhi.astype(F32), nmid.astype(F32), nlo.astype(F32), ones, pad], axis=0)
        kx_ref[j * 128:(j + 1) * 128, :] = stage.T.astype(BF16)


def _cum_forget(misc, b_forget, batch, seq):
    return pl.pallas_call(
        functools.partial(_cum_kernel, seq=seq),
        grid=(batch,),
        in_specs=[
            pl.BlockSpec((MISC_ROWS, seq), lambda b: (0, b)),
            pl.BlockSpec((8, 1), lambda b: (0, 0)),
        ],
        out_specs=[
            pl.BlockSpec((8, seq), lambda b: (0, b)),
            pl.BlockSpec((seq, 128), lambda b: (b, 0)),
        ],
        out_shape=[
            jax.ShapeDtypeStruct((8, batch * seq), F32),
            jax.ShapeDtypeStruct((batch * seq, 128), BF16),
        ],
        compiler_params=_params(("parallel",), 32),
        name="forget_cumsum",
    )(misc, b_forget.reshape(8, 1))


def _band_kernel(*refs, slopes, dist_scale, max_dist, kv_width, has_sinks, head_order, want_lw):
    refs = list(refs)
    sink_ref = refs.pop(0) if has_sinks else None
    q_ref, kp_ref, kc_ref, vp_ref, vc_ref = refs[:5]
    o_ref = refs[5]
    lw_ref = refs[6] if want_lw else None

    blk = pl.program_id(1)
    qi = lax.broadcasted_iota(I32, (BLOCK, 2 * BLOCK), 0)
    ki = lax.broadcasted_iota(I32, (BLOCK, 2 * BLOCK), 1)
    dist = qi - ki + BLOCK
    valid = (dist >= 0) & (dist <= max_dist) & ((blk > 0) | (ki >= BLOCK))
    distf = (dist * dist_scale).astype(F32)
    lo = _lane_lo(BLOCK)

    for p in range(4):
        q2 = q_ref[:, p * 128:(p + 1) * 128]
        ksl = slice(0, 128) if kv_width == 128 else slice(p * 128, (p + 1) * 128)
        k2 = jnp.concatenate([kp_ref[:, ksl], kc_ref[:, ksl]], axis=0)
        v2 = jnp.concatenate([vp_ref[:, ksl], vc_ref[:, ksl]], axis=0)
        outs, lws = [], []
        for half in range(2):
            h = head_order[2 * p + half]
            qm = jnp.where(lo if half == 0 else jnp.logical_not(lo), q2, jnp.zeros_like(q2))
            s = _nt_dot(qm, k2)
            s = s - slopes[h] * distf
            s = jnp.where(valid, s, NEG_INF)
            m = jnp.max(s, axis=1, keepdims=True)
            if has_sinks:
                sk = sink_ref[h]
                m = jnp.maximum(m, sk)
            pexp = jnp.exp(s - m)
            l = jnp.sum(pexp, axis=1, keepdims=True)
            if has_sinks:
                l = l + jnp.exp(sk - m)
            pv = jnp.dot(pexp.astype(BF16), v2, preferred_element_type=F32)
            outs.append(pv / l)
            lws.append(m + jnp.log(l))
        o_ref[:, p * 128:(p + 1) * 128] = jnp.where(lo, outs[0], outs[1]).astype(o_ref.dtype)
        if want_lw:
            lw_ref[:, p * 128:(p + 1) * 128] = jnp.where(
                lo, jnp.broadcast_to(lws[0], (BLOCK, 128)), jnp.broadcast_to(lws[1], (BLOCK, 128)))


def _banded_attention(arr, qcol, kcol, vcol, kv_width, *, slopes, dist_scale, max_dist,
                      sinks=None, head_order=PLAIN_ORDER, want_lw=False, out_dtype=F32, name):
    batch, length, _ = arr.shape
    nb = length // BLOCK
    kern = functools.partial(
        _band_kernel, slopes=slopes, dist_scale=dist_scale, max_dist=max_dist, kv_width=kv_width,
        has_sinks=sinks is not None, head_order=head_order, want_lw=want_lw)
    prev = lambda i: jnp.maximum(i - 1, 0)
    in_specs = [
        pl.BlockSpec((None, BLOCK, BRANCH_WIDTH), lambda b, i: (b, i, qcol)),
        pl.BlockSpec((None, BLOCK, kv_width), lambda b, i: (b, prev(i), kcol)),
        pl.BlockSpec((None, BLOCK, kv_width), lambda b, i: (b, i, kcol)),
        pl.BlockSpec((None, BLOCK, kv_width), lambda b, i: (b, prev(i), vcol)),
        pl.BlockSpec((None, BLOCK, kv_width), lambda b, i: (b, i, vcol)),
    ]
    args = [arr, arr, arr, arr, arr]
    if sinks is not None:
        in_specs = [pl.BlockSpec(memory_space=pltpu.SMEM)] + in_specs
        args = [sinks] + args
    o_spec = pl.BlockSpec((None, BLOCK, BRANCH_WIDTH), lambda b, i: (b, i, 0))
    o_shape = jax.ShapeDtypeStruct((batch, length, BRANCH_WIDTH), out_dtype)
    if want_lw:
        out_specs = [o_spec, o_spec]
        out_shape = [o_shape, jax.ShapeDtypeStruct((batch, length, BRANCH_WIDTH), F32)]
    else:
        out_specs, out_shape = o_spec, o_shape
    return pl.pallas_call(
        kern, grid=(batch, nb), in_specs=in_specs, out_specs=out_specs, out_shape=out_shape,
        compiler_params=_params(("parallel", "arbitrary"), 32), name=name,
    )(*args)


def _stash_scores(s, park, h):
    s_scr, mc_scr = park
    s_scr[h] = s
    mc_scr[h] = jnp.max(s, axis=0, keepdims=True)


def _fold_scores(park, h, vt, m_scr, l_scr, acc_scr):
    s_scr, mc_scr = park
    s = s_scr[h]
    m_prev = m_scr[h]
    m_new = jnp.maximum(m_prev, mc_scr[h])
    alpha = jnp.exp2(m_prev - m_new)
    pexp = jnp.exp2(s - m_new)
    l_scr[h] = alpha * l_scr[h] + jnp.sum(pexp, axis=0, keepdims=True)
    m_scr[h] = m_new
    rows = slice(h * HEAD_DIM, (h + 1) * HEAD_DIM)
    acc_scr[rows, :] = alpha * acc_scr[rows, :] + jnp.dot(vt, pexp.astype(BF16), preferred_element_type=F32)


def _init_softmax(m_scr, l_scr, acc_scr):
    m_scr[...] = jnp.full(m_scr.shape, NEG_INF, F32)
    l_scr[...] = jnp.zeros(l_scr.shape, F32)
    acc_scr[...] = jnp.zeros(acc_scr.shape, F32)


def _write_heads(o_ref, l_scr, acc_scr):
    t = CHUNK
    for p in range(4):
        l2 = jnp.concatenate(
            [jnp.broadcast_to(l_scr[2 * p], (HEAD_DIM, t)),
             jnp.broadcast_to(l_scr[2 * p + 1], (HEAD_DIM, t))], axis=0)
        o_ref[:, p * 128:(p + 1) * 128] = (acc_scr[p * 128:(p + 1) * 128, :] / l2).T.astype(o_ref.dtype)


def _row_select(rows, width, pieces):
    ridx = lax.broadcasted_iota(I32, (rows, width), 0)
    out = jnp.zeros((rows, width), F32)
    for r, piece in pieces.items():
        out = jnp.where(ridx == r, piece, out)
    return out


def _fox_kernel(k_ref, vt_ref, qt_ref, kx_ref, crow_ref, o_ref,
                w_scr, s0_scr, mc0_scr, s1_scr, mc1_scr, s2_scr, mc2_scr, m_scr, l_scr, acc_scr):
    t = CHUNK
    qi = pl.program_id(1)
    lane = lax.broadcasted_iota(I32, (t, 128), 1)
    lo = lane < HEAD_DIM

    c_terms = [x.astype(F32) for x in _split3(crow_ref[...] * LOG2E)]
    ridx = lax.broadcasted_iota(I32, (128, t), 0)
    for p in range(4):
        pieces = {}
        for i in range(3):
            pieces[KX_ONES + i] = c_terms[i][2 * p:2 * p + 1, :]
            pieces[KX_ONES + 3 + i] = c_terms[i][2 * p + 1:2 * p + 2, :]
        ext = jnp.where(ridx < KX_ONES, 1.0, _row_select(128, t, pieces))
        w_scr[p, 0:128, :] = qt_ref[p * 128:(p + 1) * 128, :]
        w_scr[p, 128:256, :] = ext.astype(BF16)

    def bias_lanes(h, half):
        first = KX_ONES + 3 * half
        return (lane == h) | (lane == 8 + h) | (lane == 16 + h) | ((lane >= first) & (lane < first + 3))

    _init_softmax(m_scr, l_scr, acc_scr)
    causal = lax.broadcasted_iota(I32, (t, t), 0) <= lax.broadcasted_iota(I32, (t, t), 1)

    parks = ((s0_scr, mc0_scr), (s1_scr, mc1_scr), (s2_scr, mc2_scr))

    def score(j, park, diagonal=False):
        start = pl.multiple_of(j * t, t)
        kx = kx_ref[pl.ds(start, t), :]
        zeros = jnp.zeros_like(kx)
        for p in range(4):
            k2 = k_ref[pl.ds(start, t), p * 128:(p + 1) * 128]
            lhs_a = jnp.concatenate(
                [jnp.where(lo, k2, zeros), jnp.where(bias_lanes(2 * p, 0), kx, zeros)], axis=1)
            lhs_b = jnp.concatenate(
                [jnp.where(lo, zeros, k2), jnp.where(bias_lanes(2 * p + 1, 1), kx, zeros)], axis=1)
            s2 = jnp.dot(jnp.concatenate([lhs_a, lhs_b], axis=0), w_scr[p], preferred_element_type=F32)
            for half in range(2):
                s = s2[half * t:(half + 1) * t, :]
                if diagonal:
                    s = jnp.where(causal, s, NEG_INF)
                _stash_scores(s, park, 2 * p + half)

    def fold(j, park):
        vt = vt_ref[j]
        for h in range(N_HEADS):
            _fold_scores(park, h, vt[h * HEAD_DIM:(h + 1) * HEAD_DIM, :], m_scr, l_scr, acc_scr)

    score(qi, parks[2], diagonal=True)

    @pl.when(qi > 0)
    def _():
        score(0, parks[0])

    fold(qi, parks[2])

    def body(i, carry):
        score(2 * i + 1, parks[1])
        fold(2 * i, parks[0])
        score(2 * i + 2, parks[0])
        fold(2 * i + 1, parks[1])
        return carry

    pairs = jnp.maximum(qi - 1, 0) // 2
    lax.fori_loop(0, pairs, body, 0)

    @pl.when((qi > 0) & (qi % 2 == 1))
    def _():
        fold(qi - 1, parks[0])

    @pl.when((qi > 0) & (qi % 2 == 0))
    def _():
        score(qi - 1, parks[1])
        fold(qi - 2, parks[0])
        fold(qi - 1, parks[1])

    _write_heads(o_ref, l_scr, acc_scr)


def _fox_attention(qkv3, tproj, kx, c_t):
    batch, seq, _ = qkv3.shape
    t = CHUNK
    nq = seq // t
    return pl.pallas_call(
        _fox_kernel,
        grid=(batch, nq),
        in_specs=[
            pl.BlockSpec((None, seq, BRANCH_WIDTH), lambda b, i: (b, 0, COL_DK // 512)),
            pl.BlockSpec((nq, BRANCH_WIDTH, t), lambda b, i: (b, ROW_DV // 512, 0)),
            pl.BlockSpec((None, BRANCH_WIDTH, t), lambda b, i: (b * nq + i, ROW_DQ // 512, 0)),
            pl.BlockSpec((seq, 128), lambda b, i: (b, 0)),
            pl.BlockSpec((8, t), lambda b, i: (0, b * nq + i)),
        ],
        out_specs=pl.BlockSpec((None, t, BRANCH_WIDTH), lambda b, i: (b, i, 0)),
        out_shape=jax.ShapeDtypeStruct((batch, seq, BRANCH_WIDTH), BF16),
        scratch_shapes=[
            pltpu.VMEM((4, 256, t), BF16),
        ] + [pltpu.VMEM((N_HEADS, t, t), F32), pltpu.VMEM((N_HEADS, 1, t), F32)] * 3 + [
            pltpu.VMEM((N_HEADS, 1, t), F32),
            pltpu.VMEM((N_HEADS, 1, t), F32),
            pltpu.VMEM((BRANCH_WIDTH, t), F32),
        ],
        compiler_params=_params(("parallel", "arbitrary"), 48),
        name="fox_attention",
    )(qkv3, tproj, tproj, kx, c_t)


COUNT_ROWS = 128


def _bf16_terms(value):
    out, rest = [], np.float32(value)
    for _ in range(3):
        bits = np.array([rest], np.float32).view(np.uint32)
        rounded = ((bits + 0x7FFF + ((bits >> 16) & 1)) & 0xFFFF0000).astype(np.uint32)
        term = rounded.view(np.float32)[0]
        out.append(float(term))
        rest = np.float32(rest - term)
    return out


SLOPES_C_LOG2 = [float(np.float32(s) * LOG2E) for s in SLOPES_C]
SLOPE_TERMS_C = [_bf16_terms(s) for s in SLOPES_C_LOG2]


def _dsa_kernel(k_ref, vt_ref, qt_ref, iqt_ref, ik_ref, iw_ref, o_ref,
                keys_scr, w_scr, s0_scr, mc0_scr, s1_scr, mc1_scr, m_scr, l_scr, acc_scr, *, topk):
    t = CHUNK
    qi = pl.program_id(1)
    n_chunks = qi + 1
    q0 = qi * t
    lane = lax.broadcasted_iota(I32, (t, 128), 1)
    lo = lane < HEAD_DIM
    qpos_row = q0 + lax.broadcasted_iota(I32, (1, t), 1)

    iws = [iw_ref[h:h + 1, :] for h in range(IDX_HEADS)]

    def score_chunk(j, carry):
        start = pl.multiple_of(j * t, t)
        ik = ik_ref[pl.ds(start, t), :]
        zeros = jnp.zeros_like(ik)
        ik_lo, ik_hi = jnp.where(lo, ik, zeros), jnp.where(lo, zeros, ik)
        lhs = jnp.concatenate(
            [jnp.concatenate([ik_lo, zeros], axis=1), jnp.concatenate([ik_hi, zeros], axis=1),
             jnp.concatenate([zeros, ik_lo], axis=1), jnp.concatenate([zeros, ik_hi], axis=1)], axis=0)
        dots = jnp.dot(lhs, iqt_ref[...], preferred_element_type=F32)
        score = jnp.zeros((t, t), F32)
        for h in range(IDX_HEADS):
            score = score + iws[h] * jnp.maximum(dots[h * t:(h + 1) * t, :], 0.0)
        bits = lax.bitcast_convert_type(score, I32)
        key = jnp.where(bits < 0, jnp.int32(INT_MIN) - bits, bits)
        kpos = start + lax.broadcasted_iota(I32, (t, 1), 0)
        keys_scr[pl.ds(start, t), :] = jnp.where(kpos <= qpos_row, key, jnp.int32(INT_MIN))
        return carry

    lax.fori_loop(0, n_chunks, score_chunk, 0)

    n_groups = n_chunks * (t // COUNT_ROWS)

    def count(cand, strict):
        def body(g, acc):
            blk = keys_scr[pl.ds(pl.multiple_of(g * COUNT_ROWS, COUNT_ROWS), COUNT_ROWS), :]
            hit = ((blk > cand) if strict else (blk >= cand)).reshape(COUNT_ROWS // 32, 32, t)
            for i in range(COUNT_ROWS // 32):
                acc = jnp.where(hit[i], acc + 1, acc)
            return acc
        acc = lax.fori_loop(0, n_groups, body, jnp.zeros((32, t), I32))
        return jnp.sum(acc, axis=0, keepdims=True)

    def bit_round(r, thr):
        cand = thr + lax.shift_left(jnp.int32(1), 31 - r)
        return jnp.where(count(cand, False) >= topk, cand, thr)

    thr = lax.fori_loop(0, 32, bit_round, jnp.full((1, t), INT_MIN, I32))
    ties_wanted = jnp.where(thr == INT_MIN, 0, topk - count(thr, True)).astype(F32)

    tpos = qpos_row.astype(F32)
    for h in range(N_HEADS):
        p, half = divmod(h, 2)
        neg_t = _split3(-(np.float32(SLOPES_C_LOG2[h]) * tpos))
        pieces = {}
        for i in range(3):
            pieces[i] = np.float32(64.0 * SLOPE_TERMS_C[h][i])
            pieces[3 + i] = np.float32(SLOPE_TERMS_C[h][i])
            pieces[6 + i] = neg_t[i].astype(F32)
        base = half * 128
        w_scr[p, base:base + HEAD_DIM, :] = qt_ref[h * HEAD_DIM:(h + 1) * HEAD_DIM, :]
        w_scr[p, base + HEAD_DIM:base + 128, :] = _row_select(HEAD_DIM, t, pieces).astype(BF16)

    krow = lax.broadcasted_iota(I32, (t, 128), 0)
    rel = lane - HEAD_DIM

    def key_extras(start):
        a = lax.shift_right_logical(start + krow, 6).astype(F32)
        b = (krow & 63).astype(F32)
        ex = jnp.where((rel >= 0) & (rel < 3), a, 0.0)
        ex = jnp.where((rel >= 3) & (rel < 6), b, ex)
        ex = jnp.where((rel >= 6) & (rel < 9), 1.0, ex)
        return ex.astype(BF16)

    _init_softmax(m_scr, l_scr, acc_scr)
    row = lax.broadcasted_iota(I32, (t, t), 0)
    col = lax.broadcasted_iota(I32, (t, t), 1)
    lower = jnp.where(col <= row, 1.0, 0.0).astype(BF16)

    parks = ((s0_scr, mc0_scr), (s1_scr, mc1_scr))

    def score(j, park, ties_seen):
        start = pl.multiple_of(j * t, t)
        key = keys_scr[pl.ds(start, t), :]
        eq = key == thr
        eqf = jnp.where(eq, 1.0, 0.0)
        prefix = jnp.dot(lower, eqf.astype(BF16), preferred_element_type=F32)
        rank = prefix - eqf + ties_seen
        sel = (key > thr) | (eq & (rank < ties_wanted))
        k_aug = jnp.where(lo, k_ref[pl.ds(start, t), :], key_extras(start))
        zeros = jnp.zeros_like(k_aug)
        lhs = jnp.concatenate(
            [jnp.concatenate([k_aug, zeros], axis=1), jnp.concatenate([zeros, k_aug], axis=1)], axis=0)
        for p in range(4):
            s2 = jnp.dot(lhs, w_scr[p], preferred_element_type=F32)
            for half in range(2):
                s = jnp.where(sel, s2[half * t:(half + 1) * t, :], NEG_INF)
                _stash_scores(s, park, 2 * p + half)
        return ties_seen + prefix[t - 1:t, :]

    def fold(j, park):
        vt = vt_ref[j]
        for h in range(N_HEADS):
            _fold_scores(park, h, vt, m_scr, l_scr, acc_scr)

    def body(i, ties_seen):
        ties_seen = score(2 * i + 1, parks[1], ties_seen)
        fold(2 * i, parks[0])
        ties_seen = score(2 * i + 2, parks[0], ties_seen)
        fold(2 * i + 1, parks[1])
        return ties_seen

    ties_seen = lax.fori_loop(0, qi // 2, body, score(0, parks[0], jnp.zeros((1, t), F32)))

    @pl.when(qi % 2 == 0)
    def _():
        fold(qi, parks[0])

    @pl.when(qi % 2 == 1)
    def _():
        score(qi, parks[1], ties_seen)
        fold(qi - 1, parks[0])
        fold(qi, parks[1])

    _write_heads(o_ref, l_scr, acc_scr)


def _dsa_attention(qkv3, tproj, misc):
    batch, seq, _ = qkv3.shape
    t = CHUNK
    nq = seq // t
    topk = min(TOPK_MAX, seq // 4)
    return pl.pallas_call(
        functools.partial(_dsa_kernel, topk=topk),
        grid=(batch, nq),
        in_specs=[
            pl.BlockSpec((None, seq, 128), lambda b, i: (b, 0, COL_CK // 128)),
            pl.BlockSpec((nq, HEAD_DIM, t), lambda b, i: (b, ROW_CV // HEAD_DIM, 0)),
            pl.BlockSpec((None, BRANCH_WIDTH, t), lambda b, i: (b * nq + i, ROW_CQ // 512, 0)),
            pl.BlockSpec((None, 256, t), lambda b, i: (b * nq + i, ROW_CIQ // 256, 0)),
            pl.BlockSpec((None, seq, 128), lambda b, i: (b, 0, COL_CIK // 128)),
            pl.BlockSpec((8, t), lambda b, i: (0, b * nq + i)),
        ],
        out_specs=pl.BlockSpec((None, t, BRANCH_WIDTH), lambda b, i: (b, i, 0)),
        out_shape=jax.ShapeDtypeStruct((batch, seq, BRANCH_WIDTH), BF16),
        scratch_shapes=[
            pltpu.VMEM((seq, t), I32),
            pltpu.VMEM((4, 256, t), BF16),
        ] + [pltpu.VMEM((N_HEADS, t, t), F32), pltpu.VMEM((N_HEADS, 1, t), F32)] * 2 + [
            pltpu.VMEM((N_HEADS, 1, t), F32),
            pltpu.VMEM((N_HEADS, 1, t), F32),
            pltpu.VMEM((BRANCH_WIDTH, t), F32),
        ],
        compiler_params=_params(("parallel", "arbitrary"), 48),
        name="dsa_attention",
    )(qkv3, tproj, tproj, tproj, qkv3, misc)


MERGE_TM = 256


def _merge_kernel(x_ref, oa_ref, ob1_ref, ob4_ref, ob16_ref, lw1_ref, lw4_ref, lw16_ref,
                  oc_ref, od_ref, wg_ref, bg_ref, wb_ref, wo_ref, g_ref, b_ref, out_ref):
    x = x_ref[...]
    xb = x.astype(BF16)
    lw1, lw4, lw16 = lw1_ref[...], lw4_ref[...], lw16_ref[...]
    top = jnp.maximum(jnp.maximum(lw1, lw4), lw16)
    w1, w4, w16 = jnp.exp(lw1 - top), jnp.exp(lw4 - top), jnp.exp(lw16 - top)
    ob = (w1 * ob1_ref[...] + w4 * ob4_ref[...] + w16 * ob16_ref[...]) / (w1 + w4 + w16)
    branches = (oa_ref[...], ob.astype(BF16), oc_ref[...], od_ref[...])
    merged = jnp.zeros((MERGE_TM, D_MODEL), F32)
    for n in range(4):
        sl = slice(n * D_MODEL, (n + 1) * D_MODEL)
        proj = jnp.dot(branches[n], wb_ref[n], preferred_element_type=F32)
        gate = jax.nn.sigmoid(jnp.dot(xb, wg_ref[:, sl], preferred_element_type=F32) + bg_ref[:, sl])
        merged = merged + gate * proj
    y = jnp.dot(merged.astype(BF16), wo_ref[...], preferred_element_type=F32)
    out_ref[...] = _layer_norm(DEEPNORM_ALPHA * x + y, g_ref[...], b_ref[...])


def _merge(x2d, oa, ob_parts, lw_parts, oc, od, wg, bg, wb, wo, g, b):
    m = x2d.shape[0]
    row = lambda w: pl.BlockSpec((MERGE_TM, w), lambda i: (i, 0))
    return pl.pallas_call(
        _merge_kernel,
        grid=(m // MERGE_TM,),
        in_specs=[row(D_MODEL)] + [row(BRANCH_WIDTH)] * 9 + [
            _resident((D_MODEL, 4 * D_MODEL)),
            _resident((1, 4 * D_MODEL)),
            _resident((4, BRANCH_WIDTH, D_MODEL)),
            _resident((D_MODEL, D_MODEL)),
            _resident((1, D_MODEL)),
            _resident((1, D_MODEL)),
        ],
        out_specs=row(D_MODEL),
        out_shape=jax.ShapeDtypeStruct((m, D_MODEL), F32),
        compiler_params=_params(("parallel",), 56),
        name="branch_merge",
    )(x2d, oa, *ob_parts, *lw_parts, oc, od, wg, bg, wb, wo, g, b)


FFN_TM = 256
FFN_CHUNK = 256
HALO = 8


def _gelu_tanh(a):
    return 0.5 * a * (1.0 + jnp.tanh(np.float32(np.sqrt(2.0 / np.pi)) * (a + 0.044715 * (a * a * a))))


def _ffn_kernel(x_ref, xh_ref, wu_ref, wg_ref, cw_ref, cb_ref, wd_ref, g_ref, b_ref, out_ref, a_scr,
                *, tiles_per_seq):
    tm = FFN_TM
    x = x_ref[...]
    xb = x.astype(BF16)
    xhb = xh_ref[...].astype(BF16)
    seq_start = (pl.program_id(0) % tiles_per_seq) == 0
    y = jnp.zeros((tm, D_MODEL), F32)
    for c in range(D_FF // FFN_CHUNK):
        sl = slice(c * FFN_CHUNK, (c + 1) * FFN_CHUNK)
        a = jnp.dot(xb, wu_ref[:, sl], preferred_element_type=F32)
        ah = jnp.dot(xhb, wu_ref[:, sl], preferred_element_type=F32)
        a_scr[0:HALO, :] = jnp.where(seq_start, 0.0, ah)
        a_scr[HALO:HALO + tm, :] = a
        conv = cb_ref[:, sl] + (cw_ref[0:1, sl] * a_scr[HALO - 2:HALO - 2 + tm, :]
                                + cw_ref[1:2, sl] * a_scr[HALO - 1:HALO - 1 + tm, :]
                                + cw_ref[2:3, sl] * a)
        gate = jnp.dot(xb, wg_ref[:, sl], preferred_element_type=F32)
        h = (_gelu_tanh(conv) * gate).astype(BF16)
        y = y + jnp.dot(h, wd_ref[sl, :], preferred_element_type=F32)
    out_ref[...] = _layer_norm(DEEPNORM_ALPHA * x + y, g_ref[...], b_ref[...])


def _ffn(x2d, wu, wg, cw, cb, wd, g, b, seq):
    m = x2d.shape[0]
    tm = FFN_TM
    return pl.pallas_call(
        functools.partial(_ffn_kernel, tiles_per_seq=seq // tm),
        grid=(m // tm,),
        in_specs=[
            pl.BlockSpec((tm, D_MODEL), lambda i: (i, 0)),
            pl.BlockSpec((HALO, D_MODEL), lambda i: (jnp.maximum(i * (tm // HALO) - 1, 0), 0)),
            _resident((D_MODEL, D_FF)),
            _resident((D_MODEL, D_FF)),
            _resident((3, D_FF)),
            _resident((1, D_FF)),
            _resident((D_FF, D_MODEL)),
            _resident((1, D_MODEL)),
            _resident((1, D_MODEL)),
        ],
        out_specs=pl.BlockSpec((tm, D_MODEL), lambda i: (i, 0)),
        out_shape=jax.ShapeDtypeStruct((m, D_MODEL), F32),
        scratch_shapes=[pltpu.VMEM((HALO + tm, FFN_CHUNK), F32)],
        compiler_params=_params(("parallel",), 56),
        name="conv_glu_ffn",
    )(x2d, x2d, wu, wg, cw, cb, wd, g, b)


def _prep_in_proj(w):
    offs = np.cumsum((0,) + IN_SPLIT_SIZES)
    (aq, ak, av, bq, bk, bv, cq, ck, cv, ciq, cik, ciw, dq, dk, dv, df) = [
        w[:, offs[i]:offs[i + 1]] for i in range(len(IN_SPLIT_SIZES))]
    scale = HEAD_DIM ** -0.5
    aq = aq.reshape(D_MODEL, N_HEADS, HEAD_DIM)[:, jnp.array(A_ORDER), :].reshape(D_MODEL, BRANCH_WIDTH)
    pad64 = jnp.zeros((D_MODEL, HEAD_DIM), F32)
    wcat = jnp.concatenate([aq * scale, bq * scale, bk, bv, dk, ak, av, ck, pad64, cik, cik], axis=1).astype(BF16)
    wmisc = jnp.concatenate([ciw, jnp.zeros((D_MODEL, 4), F32), df], axis=1).T.astype(BF16)
    wt = jnp.concatenate([dv, dq * scale, cq * scale, ciq * scale, cv], axis=1).T.astype(BF16)
    tscale = np.ones((T_ROWS, 1), np.float32)
    tscale[ROW_DQ:ROW_DQ + BRANCH_WIDTH] = LOG2E
    tscale[ROW_CQ:ROW_CQ + BRANCH_WIDTH] = LOG2E
    return wcat, wmisc, wt, jnp.asarray(tscale)


def _to_classes(a, dil):
    b, l, w = a.shape
    return a.reshape(b, l // dil, dil, w).transpose(0, 2, 1, 3).reshape(b * dil, l // dil, w)


def _from_classes(a, dil, batch):
    _, lc, w = a.shape
    return a.reshape(batch, dil, lc, w).transpose(0, 2, 1, 3).reshape(batch * lc * dil, w)


def _layer(x2d, batch, seq, w_in, b_forget, sinks, w_branch, w_gate, b_gate, w_out, ln1_g, ln1_b,
           w_up, w_ffn_gate, conv_w, conv_b, w_down, ln2_g, ln2_b):
    m = batch * seq
    wcat, wmisc, wt, tscale = _prep_in_proj(w_in)
    qkv, misc, tproj = _project(x2d, wcat, wmisc, wt, tscale)
    qkv3 = qkv.reshape(batch, seq, PROJ_WIDTH)

    oa = _banded_attention(
        qkv3, COL_AQ // 512, COL_AK // 128, COL_AV // 128, 128, slopes=SLOPES_A, dist_scale=1,
        max_dist=A_WINDOW - 1, sinks=sinks, head_order=A_ORDER, out_dtype=BF16, name="swa_attention")

    ob_parts, lw_parts = [], []
    for window, dil in B_PATTERNS:
        if dil == 1:
            arr, cols = qkv3, (COL_BQ // 512, COL_BK // 512, COL_BV // 512)
        else:
            arr, cols = _to_classes(qkv3[:, :, COL_BQ:COL_BV + 512], dil), (0, 1, 2)
        o, lw = _banded_attention(
            arr, *cols, 512, slopes=SLOPES_B, dist_scale=dil, max_dist=window // dil, want_lw=True,
            name=f"dilated_attention_{dil}")
        ob_parts.append(_from_classes(o, dil, batch))
        lw_parts.append(_from_classes(lw, dil, batch))

    oc = _dsa_attention(qkv3, tproj, misc)

    c_t, kx = _cum_forget(misc, b_forget, batch, seq)
    od = _fox_attention(qkv3, tproj, kx, c_t)

    wb = jnp.concatenate(
        [w_branch[0].reshape(N_HEADS, HEAD_DIM, D_MODEL)[jnp.array(A_ORDER)].reshape(1, BRANCH_WIDTH, D_MODEL),
         w_branch[1:]], axis=0).astype(BF16)
    x2d = _merge(
        x2d, oa.reshape(m, BRANCH_WIDTH), ob_parts, lw_parts, oc.reshape(m, BRANCH_WIDTH),
        od.reshape(m, BRANCH_WIDTH), w_gate.astype(BF16), b_gate.reshape(1, -1), wb, w_out.astype(BF16),
        ln1_g.reshape(1, -1), ln1_b.reshape(1, -1))
    return _ffn(x2d, w_up.astype(BF16), w_ffn_gate.astype(BF16), conv_w, conv_b.reshape(1, -1),
                w_down.astype(BF16), ln2_g.reshape(1, -1), ln2_b.reshape(1, -1), seq)


def kernel(x, w_in, b_forget, sinks, w_branch, w_gate, b_gate, w_out, ln1_g, ln1_b,
           w_up, w_ffn_gate, conv_w, conv_b, w_down, ln2_g, ln2_b):
    batch, seq, _ = x.shape
    x2d = x.reshape(batch * seq, D_MODEL)
    for l in range(w_in.shape[0]):
        x2d = _layer(x2d, batch, seq, w_in[l], b_forget[l], sinks[l], w_branch[l], w_gate[l], b_gate[l],
                     w_out[l], ln1_g[l], ln1_b[l], w_up[l], w_ffn_gate[l], conv_w[l], conv_b[l],
                     w_down[l], ln2_g[l], ln2_b[l])
    return x2d.reshape(batch, seq, D_MODEL)
```

```python
import functools

import numpy as np
import jax
import jax.numpy as jnp
from jax import lax
from jax.experimental import pallas as pl
from jax.experimental.pallas import tpu as pltpu

F32 = jnp.float32
BF16 = jnp.bfloat16
I32 = jnp.int32

D_MODEL = 1024
HEAD_DIM = 64
N_HEADS = 8
BRANCH_WIDTH = N_HEADS * HEAD_DIM
BLOCK = 128
NEG_INF = -1e30
INT_MIN = -2147483648
LOG2E = np.float32(1.4426950408889634)
A_WINDOW = 128
B_PATTERNS = ((128, 1), (512, 4), (2048, 16))
IDX_HEADS = 4
TOPK_MAX = 256
D_FF = 2816
LN_EPS = 1e-5
DEPTH = 2
DEEPNORM_ALPHA = (2 * DEPTH) ** 0.25
IN_SPLIT_SIZES = (512, 128, 128, 512, 512, 512, 512, 64, 64, 256, 64, 4, 512, 512, 512, 8)

A_ORDER = (0, 4, 1, 5, 2, 6, 3, 7)
PLAIN_ORDER = tuple(range(N_HEADS))

COL_AQ, COL_BQ, COL_BK, COL_BV, COL_DK = (i * 512 for i in range(5))
COL_AK, COL_AV, COL_CK, COL_CIK = 2560, 2688, 2816, 2944
PROJ_WIDTH = 3072
MISC_ROWS = 16
ROW_DV, ROW_DQ, ROW_CQ, ROW_CIQ, ROW_CV = 0, 512, 1024, 1536, 1792
T_ROWS = 1856

CHUNK = 256


def _alibi_slopes():
    s = np.exp2(-8.0 * np.arange(1, 25, dtype=np.float32) / 24).astype(np.float32)
    return [float(v) for v in s[0::3]], [float(v) for v in s[1::3]], [float(v) for v in s[2::3]]


SLOPES_A, SLOPES_B, SLOPES_C = _alibi_slopes()


def _params(semantics, vmem_mib):
    return pltpu.CompilerParams(dimension_semantics=semantics, vmem_limit_bytes=vmem_mib * 2**20)


def _resident(shape):
    nd = len(shape)
    return pl.BlockSpec(shape, lambda *_: (0,) * nd, pipeline_mode=pl.Buffered(1))


def _nt_dot(a, b):
    return lax.dot_general(a, b, (((1,), (1,)), ((), ())), preferred_element_type=F32)


def _lane_lo(rows):
    return lax.broadcasted_iota(I32, (rows, 128), 1) < HEAD_DIM


def _split3(v):
    hi = v.astype(BF16)
    r1 = v - hi.astype(F32)
    mid = r1.astype(BF16)
    lo = (r1 - mid.astype(F32)).astype(BF16)
    return hi, mid, lo


def _layer_norm(z, g, b):
    mu = jnp.mean(z, axis=-1, keepdims=True)
    zc = z - mu
    var = jnp.mean(zc * zc, axis=-1, keepdims=True)
    return zc * lax.rsqrt(var + LN_EPS) * g + b


PROJ_TM = 512
PROJ_CHUNK = 256


B_SLABS = 3 * BRANCH_WIDTH // 128


def _proj_kernel(x_ref, w_ref, wm_ref, wt_ref, ts_ref, qkv_ref, misc_ref, t_ref, c4_ref, c16_ref, b_scr):
    xb = x_ref[...].astype(BF16)
    for c in range(PROJ_WIDTH // PROJ_CHUNK):
        sl = slice(c * PROJ_CHUNK, (c + 1) * PROJ_CHUNK)
        acc = jnp.dot(xb, w_ref[:, sl], preferred_element_type=F32)
        qkv_ref[:, sl] = acc.astype(BF16)
        for half in range(PROJ_CHUNK // 128):
            col = c * PROJ_CHUNK + half * 128
            if COL_BQ <= col < COL_BV + BRANCH_WIDTH:
                b_scr[(col - COL_BQ) // 128] = acc[:, half * 128:(half + 1) * 128]
    for dil, ref in ((4, c4_ref), (16, c16_ref)):
        n = PROJ_TM // dil
        for r in range(dil):
            for s in range(B_SLABS):
                ref[r, :, s * 128:(s + 1) * 128] = b_scr[s, pl.ds(r, n, stride=dil), :].astype(BF16)
    misc_ref[...] = _nt_dot(wm_ref[...], xb)
    for c in range(PROJ_TM // CHUNK):
        acc = _nt_dot(wt_ref[...], xb[c * CHUNK:(c + 1) * CHUNK, :])
        t_ref[c] = (acc * ts_ref[...]).astype(BF16)


def _project(x2d, wcat, wmisc, wt, tscale, batch, seq):
    m = x2d.shape[0]
    tps = seq // PROJ_TM
    cls_spec = lambda dil: pl.BlockSpec(
        (None, dil, PROJ_TM // dil, 3 * BRANCH_WIDTH), lambda i: (i // tps, 0, i % tps, 0))
    cls_shape = lambda dil: jax.ShapeDtypeStruct((batch, dil, seq // dil, 3 * BRANCH_WIDTH), BF16)
    return pl.pallas_call(
        _proj_kernel,
        grid=(m // PROJ_TM,),
        in_specs=[
            pl.BlockSpec((PROJ_TM, D_MODEL), lambda i: (i, 0)),
            _resident((D_MODEL, PROJ_WIDTH)),
            _resident((MISC_ROWS, D_MODEL)),
            _resident((T_ROWS, D_MODEL)),
            _resident((T_ROWS, 1)),
        ],
        out_specs=[
            pl.BlockSpec((PROJ_TM, PROJ_WIDTH), lambda i: (i, 0)),
            pl.BlockSpec((MISC_ROWS, PROJ_TM), lambda i: (0, i)),
            pl.BlockSpec((PROJ_TM // CHUNK, T_ROWS, CHUNK), lambda i: (i, 0, 0)),
            cls_spec(4),
            cls_spec(16),
        ],
        out_shape=[
            jax.ShapeDtypeStruct((m, PROJ_WIDTH), BF16),
            jax.ShapeDtypeStruct((MISC_ROWS, m), F32),
            jax.ShapeDtypeStruct((m // CHUNK, T_ROWS, CHUNK), BF16),
            cls_shape(4),
            cls_shape(16),
        ],
        scratch_shapes=[pltpu.VMEM((B_SLABS, PROJ_TM, 128), F32)],
        compiler_params=_params(("parallel",), 48),
        name="in_proj",
    )(x2d, wcat, wmisc, wt, tscale)


KX_ONES = 24


def _cum_kernel(misc_ref, bf_ref, c_ref, kx_ref, *, seq):
    f = misc_ref[8:16, :] + bf_ref[...]
    ls = -(jnp.maximum(-f, 0.0) + jnp.log1p(jnp.exp(-jnp.abs(f))))
    row = lax.broadcasted_iota(I32, (128, 128), 0)
    col = lax.broadcasted_iota(I32, (128, 128), 1)
    upper = jnp.where(row <= col, 1.0, 0.0).astype(BF16)
    carry = jnp.zeros((8, 1), F32)
    ones = jnp.ones((8, 128), F32)
    pad = jnp.zeros((128 - 32, 128), F32)
    for j in range(seq // 128):
        hi, mid, lo = _split3(ls[:, j * 128:(j + 1) * 128])
        cs = (jnp.dot(hi, upper, preferred_element_type=F32)
              + jnp.dot(mid, upper, preferred_element_type=F32)
              + jnp.dot(lo, upper, preferred_element_type=F32))
        c = cs + carry
        c_ref[:, j * 128:(j + 1) * 128] = c
        carry = carry + cs[:, 127:128]
        nhi, nmid, nlo = _split3(-(c * LOG2E))
        stage = jnp.concatenate([nhi.astype(F32), nmid.astype(F32), nlo.astype(F32), ones, pad], axis=0)
        kx_ref[j * 128:(j + 1) * 128, :] = stage.T.astype(BF16)


def _cum_forget(misc, b_forget, batch, seq):
    return pl.pallas_call(
        functools.partial(_cum_kernel, seq=seq),
        grid=(batch,),
        in_specs=[
            pl.BlockSpec((MISC_ROWS, seq), lambda b: (0, b)),
            pl.BlockSpec((8, 1), lambda b: (0, 0)),
        ],
        out_specs=[
            pl.BlockSpec((8, seq), lambda b: (0, b)),
            pl.BlockSpec((seq, 128), lambda b: (b, 0)),
        ],
        out_shape=[
            jax.ShapeDtypeStruct((8, batch * seq), F32),
            jax.ShapeDtypeStruct((batch * seq, 128), BF16),
        ],
        compiler_params=_params(("parallel",), 32),
        name="forget_cumsum",
    )(misc, b_forget.reshape(8, 1))


def _band_kernel(*refs, slopes, dist_scale, max_dist, kv_width, has_sinks, head_order, want_lw):
    refs = list(refs)
    sink_ref = refs.pop(0) if has_sinks else None
    q_ref, kp_ref, kc_ref, vp_ref, vc_ref = refs[:5]
    o_ref = refs[5]
    lw_ref = refs[6] if want_lw else None

    blk = pl.program_id(1)
    qi = lax.broadcasted_iota(I32, (BLOCK, 2 * BLOCK), 0)
    ki = lax.broadcasted_iota(I32, (BLOCK, 2 * BLOCK), 1)
    dist = qi - ki + BLOCK
    valid = (dist >= 0) & (dist <= max_dist) & ((blk > 0) | (ki >= BLOCK))
    distf = (dist * dist_scale).astype(F32)
    lo = _lane_lo(BLOCK)

    for p in range(4):
        q2 = q_ref[:, p * 128:(p + 1) * 128]
        ksl = slice(0, 128) if kv_width == 128 else slice(p * 128, (p + 1) * 128)
        k2 = jnp.concatenate([kp_ref[:, ksl], kc_ref[:, ksl]], axis=0)
        v2 = jnp.concatenate([vp_ref[:, ksl], vc_ref[:, ksl]], axis=0)
        outs, lws = [], []
        for half in range(2):
            h = head_order[2 * p + half]
            qm = jnp.where(lo if half == 0 else jnp.logical_not(lo), q2, jnp.zeros_like(q2))
            s = _nt_dot(qm, k2)
            s = s - slopes[h] * distf
            s = jnp.where(valid, s, NEG_INF)
            m = jnp.max(s, axis=1, keepdims=True)
            if has_sinks:
                sk = sink_ref[h]
                m = jnp.maximum(m, sk)
            pexp = jnp.exp(s - m)
            l = jnp.sum(pexp, axis=1, keepdims=True)
            if has_sinks:
                l = l + jnp.exp(sk - m)
            pv = jnp.dot(pexp.astype(BF16), v2, preferred_element_type=F32)
            outs.append(pv / l)
            lws.append(m + jnp.log(l))
        o_ref[:, p * 128:(p + 1) * 128] = jnp.where(lo, outs[0], outs[1]).astype(o_ref.dtype)
        if want_lw:
            lw_ref[:, p * 128:(p + 1) * 128] = jnp.where(
                lo, jnp.broadcast_to(lws[0], (BLOCK, 128)), jnp.broadcast_to(lws[1], (BLOCK, 128)))


def _banded_attention(arr, qcol, kcol, vcol, kv_width, *, slopes, dist_scale, max_dist,
                      sinks=None, head_order=PLAIN_ORDER, want_lw=False, out_dtype=F32, name):
    batch, length, _ = arr.shape
    nb = length // BLOCK
    kern = functools.partial(
        _band_kernel, slopes=slopes, dist_scale=dist_scale, max_dist=max_dist, kv_width=kv_width,
        has_sinks=sinks is not None, head_order=head_order, want_lw=want_lw)
    prev = lambda i: jnp.maximum(i - 1, 0)
    in_specs = [
        pl.BlockSpec((None, BLOCK, BRANCH_WIDTH), lambda b, i: (b, i, qcol)),
        pl.BlockSpec((None, BLOCK, kv_width), lambda b, i: (b, prev(i), kcol)),
        pl.BlockSpec((None, BLOCK, kv_width), lambda b, i: (b, i, kcol)),
        pl.BlockSpec((None, BLOCK, kv_width), lambda b, i: (b, prev(i), vcol)),
        pl.BlockSpec((None, BLOCK, kv_width), lambda b, i: (b, i, vcol)),
    ]
    args = [arr, arr, arr, arr, arr]
    if sinks is not None:
        in_specs = [pl.BlockSpec(memory_space=pltpu.SMEM)] + in_specs
        args = [sinks] + args
    o_spec = pl.BlockSpec((None, BLOCK, BRANCH_WIDTH), lambda b, i: (b, i, 0))
    o_shape = jax.ShapeDtypeStruct((batch, length, BRANCH_WIDTH), out_dtype)
    if want_lw:
        out_specs = [o_spec, o_spec]
        out_shape = [o_shape, jax.ShapeDtypeStruct((batch, length, BRANCH_WIDTH), F32)]
    else:
        out_specs, out_shape = o_spec, o_shape
    return pl.pallas_call(
        kern, grid=(batch, nb), in_specs=in_specs, out_specs=out_specs, out_shape=out_shape,
        compiler_params=_params(("parallel", "arbitrary"), 32), name=name,
    )(*args)


def _stash_scores(s, park, h):
    s_scr, mc_scr = park
    s_scr[h] = s
    mc_scr[h] = jnp.max(s, axis=0, keepdims=True)


def _fold_scores(park, h, vt, m_scr, l_scr, acc_scr):
    s_scr, mc_scr = park
    s = s_scr[h]
    m_prev = m_scr[h]
    m_new = jnp.maximum(m_prev, mc_scr[h])
    alpha = jnp.exp2(m_prev - m_new)
    pexp = jnp.exp2(s - m_new)
    l_scr[h] = alpha * l_scr[h] + jnp.sum(pexp, axis=0, keepdims=True)
    m_scr[h] = m_new
    rows = slice(h * HEAD_DIM, (h + 1) * HEAD_DIM)
    acc_scr[rows, :] = alpha * acc_scr[rows, :] + jnp.dot(vt, pexp.astype(BF16), preferred_element_type=F32)


def _init_softmax(m_scr, l_scr, acc_scr):
    m_scr[...] = jnp.full(m_scr.shape, NEG_INF, F32)
    l_scr[...] = jnp.zeros(l_scr.shape, F32)
    acc_scr[...] = jnp.zeros(acc_scr.shape, F32)


def _write_heads(o_ref, l_scr, acc_scr):
    t = CHUNK
    for p in range(4):
        l2 = jnp.concatenate(
            [jnp.broadcast_to(l_scr[2 * p], (HEAD_DIM, t)),
             jnp.broadcast_to(l_scr[2 * p + 1], (HEAD_DIM, t))], axis=0)
        o_ref[:, p * 128:(p + 1) * 128] = (acc_scr[p * 128:(p + 1) * 128, :] / l2).T.astype(o_ref.dtype)


def _row_select(rows, width, pieces):
    ridx = lax.broadcasted_iota(I32, (rows, width), 0)
    out = jnp.zeros((rows, width), F32)
    for r, piece in pieces.items():
        out = jnp.where(ridx == r, piece, out)
    return out


def _fox_kernel(k_ref, vt_ref, qt_ref, kx_ref, crow_ref, o_ref,
                w_scr, s0_scr, mc0_scr, s1_scr, mc1_scr, s2_scr, mc2_scr, m_scr, l_scr, acc_scr):
    t = CHUNK
    qi = pl.program_id(1)
    lane = lax.broadcasted_iota(I32, (t, 128), 1)
    lo = lane < HEAD_DIM

    c_terms = [x.astype(F32) for x in _split3(crow_ref[...] * LOG2E)]
    ridx = lax.broadcasted_iota(I32, (128, t), 0)
    for p in range(4):
        pieces = {}
        for i in range(3):
            pieces[KX_ONES + i] = c_terms[i][2 * p:2 * p + 1, :]
            pieces[KX_ONES + 3 + i] = c_terms[i][2 * p + 1:2 * p + 2, :]
        ext = jnp.where(ridx < KX_ONES, 1.0, _row_select(128, t, pieces))
        w_scr[p, 0:128, :] = qt_ref[p * 128:(p + 1) * 128, :]
        w_scr[p, 128:256, :] = ext.astype(BF16)

    def bias_lanes(h, half):
        first = KX_ONES + 3 * half
        return (lane == h) | (lane == 8 + h) | (lane == 16 + h) | ((lane >= first) & (lane < first + 3))

    _init_softmax(m_scr, l_scr, acc_scr)
    causal = lax.broadcasted_iota(I32, (t, t), 0) <= lax.broadcasted_iota(I32, (t, t), 1)

    parks = ((s0_scr, mc0_scr), (s1_scr, mc1_scr), (s2_scr, mc2_scr))

    def score(j, park, diagonal=False):
        start = pl.multiple_of(j * t, t)
        kx = kx_ref[pl.ds(start, t), :]
        zeros = jnp.zeros_like(kx)
        for p in range(4):
            k2 = k_ref[pl.ds(start, t), p * 128:(p + 1) * 128]
            lhs_a = jnp.concatenate(
                [jnp.where(lo, k2, zeros), jnp.where(bias_lanes(2 * p, 0), kx, zeros)], axis=1)
            lhs_b = jnp.concatenate(
                [jnp.where(lo, zeros, k2), jnp.where(bias_lanes(2 * p + 1, 1), kx, zeros)], axis=1)
            s2 = jnp.dot(jnp.concatenate([lhs_a, lhs_b], axis=0), w_scr[p], preferred_element_type=F32)
            for half in range(2):
                s = s2[half * t:(half + 1) * t, :]
                if diagonal:
                    s = jnp.where(causal, s, NEG_INF)
                _stash_scores(s, park, 2 * p + half)

    def fold(j, park):
        vt = vt_ref[j]
        for h in range(N_HEADS):
            _fold_scores(park, h, vt[h * HEAD_DIM:(h + 1) * HEAD_DIM, :], m_scr, l_scr, acc_scr)

    score(qi, parks[2], diagonal=True)

    @pl.when(qi > 0)
    def _():
        score(0, parks[0])

    fold(qi, parks[2])

    def body(i, carry):
        score(2 * i + 1, parks[1])
        fold(2 * i, parks[0])
        score(2 * i + 2, parks[0])
        fold(2 * i + 1, parks[1])
        return carry

    pairs = jnp.maximum(qi - 1, 0) // 2
    lax.fori_loop(0, pairs, body, 0)

    @pl.when((qi > 0) & (qi % 2 == 1))
    def _():
        fold(qi - 1, parks[0])

    @pl.when((qi > 0) & (qi % 2 == 0))
    def _():
        score(qi - 1, parks[1])
        fold(qi - 2, parks[0])
        fold(qi - 1, parks[1])

    _write_heads(o_ref, l_scr, acc_scr)


def _fox_attention(qkv3, tproj, kx, c_t):
    batch, seq, _ = qkv3.shape
    t = CHUNK
    nq = seq // t
    return pl.pallas_call(
        _fox_kernel,
        grid=(batch, nq),
        in_specs=[
            pl.BlockSpec((None, seq, BRANCH_WIDTH), lambda b, i: (b, 0, COL_DK // 512)),
            pl.BlockSpec((nq, BRANCH_WIDTH, t), lambda b, i: (b, ROW_DV // 512, 0)),
            pl.BlockSpec((None, BRANCH_WIDTH, t), lambda b, i: (b * nq + i, ROW_DQ // 512, 0)),
            pl.BlockSpec((seq, 128), lambda b, i: (b, 0)),
            pl.BlockSpec((8, t), lambda b, i: (0, b * nq + i)),
        ],
        out_specs=pl.BlockSpec((None, t, BRANCH_WIDTH), lambda b, i: (b, i, 0)),
        out_shape=jax.ShapeDtypeStruct((batch, seq, BRANCH_WIDTH), BF16),
        scratch_shapes=[
            pltpu.VMEM((4, 256, t), BF16),
        ] + [pltpu.VMEM((N_HEADS, t, t), F32), pltpu.VMEM((N_HEADS, 1, t), F32)] * 3 + [
            pltpu.VMEM((N_HEADS, 1, t), F32),
            pltpu.VMEM((N_HEADS, 1, t), F32),
            pltpu.VMEM((BRANCH_WIDTH, t), F32),
        ],
        compiler_params=_params(("parallel", "arbitrary"), 48),
        name="fox_attention",
    )(qkv3, tproj, tproj, kx, c_t)


COUNT_ROWS = 128


def _bf16_terms(value):
    out, rest = [], np.float32(value)
    for _ in range(3):
        bits = np.array([rest], np.float32).view(np.uint32)
        rounded = ((bits + 0x7FFF + ((bits >> 16) & 1)) & 0xFFFF0000).astype(np.uint32)
        term = rounded.view(np.float32)[0]
        out.append(float(term))
        rest = np.float32(rest - term)
    return out


SLOPES_C_LOG2 = [float(np.float32(s) * LOG2E) for s in SLOPES_C]
SLOPE_TERMS_C = [_bf16_terms(s) for s in SLOPES_C_LOG2]


def _dsa_kernel(k_ref, vt_ref, qt_ref, iqt_ref, ik_ref, iw_ref, o_ref,
                keys_scr, w_scr, s0_scr, mc0_scr, s1_scr, mc1_scr, m_scr, l_scr, acc_scr, *, topk):
    t = CHUNK
    qi = pl.program_id(1)
    n_chunks = qi + 1
    q0 = qi * t
    lane = lax.broadcasted_iota(I32, (t, 128), 1)
    lo = lane < HEAD_DIM
    qpos_row = q0 + lax.broadcasted_iota(I32, (1, t), 1)

    iws = [iw_ref[h:h + 1, :] for h in range(IDX_HEADS)]

    def score_chunk(j, carry):
        start = pl.multiple_of(j * t, t)
        ik = ik_ref[pl.ds(start, t), :]
        zeros = jnp.zeros_like(ik)
        ik_lo, ik_hi = jnp.where(lo, ik, zeros), jnp.where(lo, zeros, ik)
        lhs = jnp.concatenate(
            [jnp.concatenate([ik_lo, zeros], axis=1), jnp.concatenate([ik_hi, zeros], axis=1),
             jnp.concatenate([zeros, ik_lo], axis=1), jnp.concatenate([zeros, ik_hi], axis=1)], axis=0)
        dots = jnp.dot(lhs, iqt_ref[...], preferred_element_type=F32)
        score = jnp.zeros((t, t), F32)
        for h in range(IDX_HEADS):
            score = score + iws[h] * jnp.maximum(dots[h * t:(h + 1) * t, :], 0.0)
        bits = lax.bitcast_convert_type(score, I32)
        key = jnp.where(bits < 0, jnp.int32(INT_MIN) - bits, bits)
        kpos = start + lax.broadcasted_iota(I32, (t, 1), 0)
        keys_scr[pl.ds(start, t), :] = jnp.where(kpos <= qpos_row, key, jnp.int32(INT_MIN))
        return carry

    lax.fori_loop(0, n_chunks, score_chunk, 0)

    n_groups = n_chunks * (t // COUNT_ROWS)

    def count(cand, strict):
        def body(g, acc):
            blk = keys_scr[pl.ds(pl.multiple_of(g * COUNT_ROWS, COUNT_ROWS), COUNT_ROWS), :]
            hit = ((blk > cand) if strict else (blk >= cand)).reshape(COUNT_ROWS // 32, 32, t)
            for i in range(COUNT_ROWS // 32):
                acc = jnp.where(hit[i], acc + 1, acc)
            return acc
        acc = lax.fori_loop(0, n_groups, body, jnp.zeros((32, t), I32))
        return jnp.sum(acc, axis=0, keepdims=True)

    def bit_round(r, thr):
        cand = thr + lax.shift_left(jnp.int32(1), 31 - r)
        return jnp.where(count(cand, False) >= topk, cand, thr)

    thr = lax.fori_loop(0, 32, bit_round, jnp.full((1, t), INT_MIN, I32))
    ties_wanted = jnp.where(thr == INT_MIN, 0, topk - count(thr, True)).astype(F32)

    tpos = qpos_row.astype(F32)
    for h in range(N_HEADS):
        p, half = divmod(h, 2)
        neg_t = _split3(-(np.float32(SLOPES_C_LOG2[h]) * tpos))
        pieces = {}
        for i in range(3):
            pieces[i] = np.float32(64.0 * SLOPE_TERMS_C[h][i])
            pieces[3 + i] = np.float32(SLOPE_TERMS_C[h][i])
            pieces[6 + i] = neg_t[i].astype(F32)
        base = half * 128
        w_scr[p, base:base + HEAD_DIM, :] = qt_ref[h * HEAD_DIM:(h + 1) * HEAD_DIM, :]
        w_scr[p, base + HEAD_DIM:base + 128, :] = _row_select(HEAD_DIM, t, pieces).astype(BF16)

    krow = lax.broadcasted_iota(I32, (t, 128), 0)
    rel = lane - HEAD_DIM

    def key_extras(start):
        a = lax.shift_right_logical(start + krow, 6).astype(F32)
        b = (krow & 63).astype(F32)
        ex = jnp.where((rel >= 0) & (rel < 3), a, 0.0)
        ex = jnp.where((rel >= 3) & (rel < 6), b, ex)
        ex = jnp.where((rel >= 6) & (rel < 9), 1.0, ex)
        return ex.astype(BF16)

    _init_softmax(m_scr, l_scr, acc_scr)
    row = lax.broadcasted_iota(I32, (t, t), 0)
    col = lax.broadcasted_iota(I32, (t, t), 1)
    lower = jnp.where(col <= row, 1.0, 0.0).astype(BF16)

    parks = ((s0_scr, mc0_scr), (s1_scr, mc1_scr))

    def score(j, park, ties_seen):
        start = pl.multiple_of(j * t, t)
        key = keys_scr[pl.ds(start, t), :]
        eq = key == thr
        eqf = jnp.where(eq, 1.0, 0.0)
        prefix = jnp.dot(lower, eqf.astype(BF16), preferred_element_type=F32)
        rank = prefix - eqf + ties_seen
        sel = (key > thr) | (eq & (rank < ties_wanted))
        k_aug = jnp.where(lo, k_ref[pl.ds(start, t), :], key_extras(start))
        zeros = jnp.zeros_like(k_aug)
        lhs = jnp.concatenate(
            [jnp.concatenate([k_aug, zeros], axis=1), jnp.concatenate([zeros, k_aug], axis=1)], axis=0)
        for p in range(4):
            s2 = jnp.dot(lhs, w_scr[p], preferred_element_type=F32)
            for half in range(2):
                s = jnp.where(sel, s2[half * t:(half + 1) * t, :], NEG_INF)
                _stash_scores(s, park, 2 * p + half)
        return ties_seen + prefix[t - 1:t, :]

    def fold(j, park):
        vt = vt_ref[j]
        for h in range(N_HEADS):
            _fold_scores(park, h, vt, m_scr, l_scr, acc_scr)

    def body(i, ties_seen):
        ties_seen = score(2 * i + 1, parks[1], ties_seen)
        fold(2 * i, parks[0])
        ties_seen = score(2 * i + 2, parks[0], ties_seen)
        fold(2 * i + 1, parks[1])
        return ties_seen

    ties_seen = lax.fori_loop(0, qi // 2, body, score(0, parks[0], jnp.zeros((1, t), F32)))

    @pl.when(qi % 2 == 0)
    def _():
        fold(qi, parks[0])

    @pl.when(qi % 2 == 1)
    def _():
        score(qi, parks[1], ties_seen)
        fold(qi - 1, parks[0])
        fold(qi, parks[1])

    _write_heads(o_ref, l_scr, acc_scr)


def _dsa_attention(qkv3, tproj, misc):
    batch, seq, _ = qkv3.shape
    t = CHUNK
    nq = seq // t
    topk = min(TOPK_MAX, seq // 4)
    return pl.pallas_call(
        functools.partial(_dsa_kernel, topk=topk),
        grid=(batch, nq),
        in_specs=[
            pl.BlockSpec((None, seq, 128), lambda b, i: (b, 0, COL_CK // 128)),
            pl.BlockSpec((nq, HEAD_DIM, t), lambda b, i: (b, ROW_CV // HEAD_DIM, 0)),
            pl.BlockSpec((None, BRANCH_WIDTH, t), lambda b, i: (b * nq + i, ROW_CQ // 512, 0)),
            pl.BlockSpec((None, 256, t), lambda b, i: (b * nq + i, ROW_CIQ // 256, 0)),
            pl.BlockSpec((None, seq, 128), lambda b, i: (b, 0, COL_CIK // 128)),
            pl.BlockSpec((8, t), lambda b, i: (0, b * nq + i)),
        ],
        out_specs=pl.BlockSpec((None, t, BRANCH_WIDTH), lambda b, i: (b, i, 0)),
        out_shape=jax.ShapeDtypeStruct((batch, seq, BRANCH_WIDTH), BF16),
        scratch_shapes=[
            pltpu.VMEM((seq, t), I32),
            pltpu.VMEM((4, 256, t), BF16),
        ] + [pltpu.VMEM((N_HEADS, t, t), F32), pltpu.VMEM((N_HEADS, 1, t), F32)] * 2 + [
            pltpu.VMEM((N_HEADS, 1, t), F32),
            pltpu.VMEM((N_HEADS, 1, t), F32),
            pltpu.VMEM((BRANCH_WIDTH, t), F32),
        ],
        compiler_params=_params(("parallel", "arbitrary"), 48),
        name="dsa_attention",
    )(qkv3, tproj, tproj, tproj, qkv3, misc)


MERGE_TM = 256


def _interleave_classes(ref, dil, scr):
    n = MERGE_TM // dil
    for r in range(dil):
        for s in range(BRANCH_WIDTH // 128):
            scr[s, pl.ds(r, n, stride=dil), :] = ref[r, :, s * 128:(s + 1) * 128]
    return jnp.concatenate([scr[s] for s in range(BRANCH_WIDTH // 128)], axis=1)


def _merge_kernel(x_ref, oa_ref, ob1_ref, ob4_ref, ob16_ref, lw1_ref, lw4_ref, lw16_ref,
                  oc_ref, od_ref, wg_ref, bg_ref, wb_ref, wo_ref, g_ref, b_ref, out_ref, cls_scr):
    x = x_ref[...]
    xb = x.astype(BF16)
    ob4 = _interleave_classes(ob4_ref, 4, cls_scr.at[0])
    ob16 = _interleave_classes(ob16_ref, 16, cls_scr.at[1])
    lw4 = _interleave_classes(lw4_ref, 4, cls_scr.at[2])
    lw16 = _interleave_classes(lw16_ref, 16, cls_scr.at[3])
    lw1 = lw1_ref[...]
    top = jnp.maximum(jnp.maximum(lw1, lw4), lw16)
    w1, w4, w16 = jnp.exp(lw1 - top), jnp.exp(lw4 - top), jnp.exp(lw16 - top)
    ob = (w1 * ob1_ref[...] + w4 * ob4 + w16 * ob16) / (w1 + w4 + w16)
    branches = (oa_ref[...], ob.astype(BF16), oc_ref[...], od_ref[...])
    merged = jnp.zeros((MERGE_TM, D_MODEL), F32)
    for n in range(4):
        sl = slice(n * D_MODEL, (n + 1) * D_MODEL)
        proj = jnp.dot(branches[n], wb_ref[n], preferred_element_type=F32)
        gate = jax.nn.sigmoid(jnp.dot(xb, wg_ref[:, sl], preferred_element_type=F32) + bg_ref[:, sl])
        merged = merged + gate * proj
    y = jnp.dot(merged.astype(BF16), wo_ref[...], preferred_element_type=F32)
    out_ref[...] = _layer_norm(DEEPNORM_ALPHA * x + y, g_ref[...], b_ref[...])


def _merge(x2d, oa, ob_parts, lw_parts, oc, od, wg, bg, wb, wo, g, b, seq):
    m = x2d.shape[0]
    tps = seq // MERGE_TM
    row = lambda w: pl.BlockSpec((MERGE_TM, w), lambda i: (i, 0))
    cls = lambda dil: pl.BlockSpec(
        (None, dil, MERGE_TM // dil, BRANCH_WIDTH), lambda i: (i // tps, 0, i % tps, 0))
    groups = [row(BRANCH_WIDTH), cls(4), cls(16)]
    return pl.pallas_call(
        _merge_kernel,
        grid=(m // MERGE_TM,),
        in_specs=[row(D_MODEL), row(BRANCH_WIDTH)] + groups + groups + [row(BRANCH_WIDTH)] * 2 + [
            _resident((D_MODEL, 4 * D_MODEL)),
            _resident((1, 4 * D_MODEL)),
            _resident((4, BRANCH_WIDTH, D_MODEL)),
            _resident((D_MODEL, D_MODEL)),
            _resident((1, D_MODEL)),
            _resident((1, D_MODEL)),
        ],
        out_specs=row(D_MODEL),
        out_shape=jax.ShapeDtypeStruct((m, D_MODEL), F32),
        scratch_shapes=[pltpu.VMEM((4, BRANCH_WIDTH // 128, MERGE_TM, 128), F32)],
        compiler_params=_params(("parallel",), 56),
        name="branch_merge",
    )(x2d, oa, *ob_parts, *lw_parts, oc, od, wg, bg, wb, wo, g, b)


FFN_TM = 256
FFN_CHUNK = 256
HALO = 8


def _gelu_tanh(a):
    return 0.5 * a * (1.0 + jnp.tanh(np.float32(np.sqrt(2.0 / np.pi)) * (a + 0.044715 * (a * a * a))))


def _ffn_kernel(x_ref, xh_ref, wu_ref, wg_ref, cw_ref, cb_ref, wd_ref, g_ref, b_ref, out_ref, a_scr,
                *, tiles_per_seq):
    tm = FFN_TM
    x = x_ref[...]
    xb = x.astype(BF16)
    xhb = xh_ref[...].astype(BF16)
    seq_start = (pl.program_id(0) % tiles_per_seq) == 0
    y = jnp.zeros((tm, D_MODEL), F32)
    for c in range(D_FF // FFN_CHUNK):
        sl = slice(c * FFN_CHUNK, (c + 1) * FFN_CHUNK)
        a = jnp.dot(xb, wu_ref[:, sl], preferred_element_type=F32)
        ah = jnp.dot(xhb, wu_ref[:, sl], preferred_element_type=F32)
        a_scr[0:HALO, :] = jnp.where(seq_start, 0.0, ah)
        a_scr[HALO:HALO + tm, :] = a
        conv = cb_ref[:, sl] + (cw_ref[0:1, sl] * a_scr[HALO - 2:HALO - 2 + tm, :]
                                + cw_ref[1:2, sl] * a_scr[HALO - 1:HALO - 1 + tm, :]
                                + cw_ref[2:3, sl] * a)
        gate = jnp.dot(xb, wg_ref[:, sl], preferred_element_type=F32)
        h = (_gelu_tanh(conv) * gate).astype(BF16)
        y = y + jnp.dot(h, wd_ref[sl, :], preferred_element_type=F32)
    out_ref[...] = _layer_norm(DEEPNORM_ALPHA * x + y, g_ref[...], b_ref[...])


def _ffn(x2d, wu, wg, cw, cb, wd, g, b, seq):
    m = x2d.shape[0]
    tm = FFN_TM
    return pl.pallas_call(
        functools.partial(_ffn_kernel, tiles_per_seq=seq // tm),
        grid=(m // tm,),
        in_specs=[
            pl.BlockSpec((tm, D_MODEL), lambda i: (i, 0)),
            pl.BlockSpec((HALO, D_MODEL), lambda i: (jnp.maximum(i * (tm // HALO) - 1, 0), 0)),
            _resident((D_MODEL, D_FF)),
            _resident((D_MODEL, D_FF)),
            _resident((3, D_FF)),
            _resident((1, D_FF)),
            _resident((D_FF, D_MODEL)),
            _resident((1, D_MODEL)),
            _resident((1, D_MODEL)),
        ],
        out_specs=pl.BlockSpec((tm, D_MODEL), lambda i: (i, 0)),
        out_shape=jax.ShapeDtypeStruct((m, D_MODEL), F32),
        scratch_shapes=[pltpu.VMEM((HALO + tm, FFN_CHUNK), F32)],
        compiler_params=_params(("parallel",), 56),
        name="conv_glu_ffn",
    )(x2d, x2d, wu, wg, cw, cb, wd, g, b)


def _prep_in_proj(w):
    offs = np.cumsum((0,) + IN_SPLIT_SIZES)
    (aq, ak, av, bq, bk, bv, cq, ck, cv, ciq, cik, ciw, dq, dk, dv, df) = [
        w[:, offs[i]:offs[i + 1]] for i in range(len(IN_SPLIT_SIZES))]
    scale = HEAD_DIM ** -0.5
    aq = aq.reshape(D_MODEL, N_HEADS, HEAD_DIM)[:, jnp.array(A_ORDER), :].reshape(D_MODEL, BRANCH_WIDTH)
    pad64 = jnp.zeros((D_MODEL, HEAD_DIM), F32)
    wcat = jnp.concatenate([aq * scale, bq * scale, bk, bv, dk, ak, av, ck, pad64, cik, cik], axis=1).astype(BF16)
    wmisc = jnp.concatenate([ciw, jnp.zeros((D_MODEL, 4), F32), df], axis=1).T.astype(BF16)
    wt = jnp.concatenate([dv, dq * scale, cq * scale, ciq * scale, cv], axis=1).T.astype(BF16)
    tscale = np.ones((T_ROWS, 1), np.float32)
    tscale[ROW_DQ:ROW_DQ + BRANCH_WIDTH] = LOG2E
    tscale[ROW_CQ:ROW_CQ + BRANCH_WIDTH] = LOG2E
    return wcat, wmisc, wt, jnp.asarray(tscale)


def _layer(x2d, batch, seq, w_in, b_forget, sinks, w_branch, w_gate, b_gate, w_out, ln1_g, ln1_b,
           w_up, w_ffn_gate, conv_w, conv_b, w_down, ln2_g, ln2_b):
    m = batch * seq
    wcat, wmisc, wt, tscale = _prep_in_proj(w_in)
    qkv, misc, tproj, cls4, cls16 = _project(x2d, wcat, wmisc, wt, tscale, batch, seq)
    qkv3 = qkv.reshape(batch, seq, PROJ_WIDTH)
    classes = {4: cls4, 16: cls16}

    oa = _banded_attention(
        qkv3, COL_AQ // 512, COL_AK // 128, COL_AV // 128, 128, slopes=SLOPES_A, dist_scale=1,
        max_dist=A_WINDOW - 1, sinks=sinks, head_order=A_ORDER, out_dtype=BF16, name="swa_attention")

    ob_parts, lw_parts = [], []
    for window, dil in B_PATTERNS:
        if dil == 1:
            arr, cols = qkv3, (COL_BQ // 512, COL_BK // 512, COL_BV // 512)
        else:
            arr, cols = classes[dil].reshape(batch * dil, seq // dil, 3 * BRANCH_WIDTH), (0, 1, 2)
        o, lw = _banded_attention(
            arr, *cols, 512, slopes=SLOPES_B, dist_scale=dil, max_dist=window // dil, want_lw=True,
            name=f"dilated_attention_{dil}")
        shape = (m, BRANCH_WIDTH) if dil == 1 else (batch, dil, seq // dil, BRANCH_WIDTH)
        ob_parts.append(o.reshape(shape))
        lw_parts.append(lw.reshape(shape))

    oc = _dsa_attention(qkv3, tproj, misc)

    c_t, kx = _cum_forget(misc, b_forget, batch, seq)
    od = _fox_attention(qkv3, tproj, kx, c_t)

    wb = jnp.concatenate(
        [w_branch[0].reshape(N_HEADS, HEAD_DIM, D_MODEL)[jnp.array(A_ORDER)].reshape(1, BRANCH_WIDTH, D_MODEL),
         w_branch[1:]], axis=0).astype(BF16)
    x2d = _merge(
        x2d, oa.reshape(m, BRANCH_WIDTH), ob_parts, lw_parts, oc.reshape(m, BRANCH_WIDTH),
        od.reshape(m, BRANCH_WIDTH), w_gate.astype(BF16), b_gate.reshape(1, -1), wb, w_out.astype(BF16),
        ln1_g.reshape(1, -1), ln1_b.reshape(1, -1), seq)
    return _ffn(x2d, w_up.astype(BF16), w_ffn_gate.astype(BF16), conv_w, conv_b.reshape(1, -1),
                w_down.astype(BF16), ln2_g.reshape(1, -1), ln2_b.reshape(1, -1), seq)


def kernel(x, w_in, b_forget, sinks, w_branch, w_gate, b_gate, w_out, ln1_g, ln1_b,
           w_up, w_ffn_gate, conv_w, conv_b, w_down, ln2_g, ln2_b):
    batch, seq, _ = x.shape
    x2d = x.reshape(batch * seq, D_MODEL)
    for l in range(w_in.shape[0]):
        x2d = _layer(x2d, batch, seq, w_in[l], b_forget[l], sinks[l], w_branch[l], w_gate[l], b_gate[l],
                     w_out[l], ln1_g[l], ln1_b[l], w_up[l], w_ffn_gate[l], conv_w[l], conv_b[l],
                     w_down[l], ln2_g[l], ln2_b[l])
    return x2d.reshape(batch, seq, D_MODEL)
```

```python
import functools

import numpy as np
import jax
import jax.numpy as jnp
from jax import lax
from jax.experimental import pallas as pl
from jax.experimental.pallas import tpu as pltpu

F32 = jnp.float32
BF16 = jnp.bfloat16
I32 = jnp.int32

D_MODEL = 1024
HEAD_DIM = 64
N_HEADS = 8
BRANCH_WIDTH = N_HEADS * HEAD_DIM
BLOCK = 128
NEG_INF = -1e30
INT_MIN = -2147483648
LOG2E = np.float32(1.4426950408889634)
A_WINDOW = 128
B_PATTERNS = ((128, 1), (512, 4), (2048, 16))
IDX_HEADS = 4
TOPK_MAX = 256
D_FF = 2816
LN_EPS = 1e-5
DEPTH = 2
DEEPNORM_ALPHA = (2 * DEPTH) ** 0.25
IN_SPLIT_SIZES = (512, 128, 128, 512, 512, 512, 512, 64, 64, 256, 64, 4, 512, 512, 512, 8)

A_ORDER = (0, 4, 1, 5, 2, 6, 3, 7)
PLAIN_ORDER = tuple(range(N_HEADS))

COL_AQ, COL_BQ, COL_BK, COL_BV, COL_DK = (i * 512 for i in range(5))
COL_AK, COL_AV, COL_CK, COL_CIK = 2560, 2688, 2816, 2944
PROJ_WIDTH = 3072
MISC_ROWS = 16
ROW_DV, ROW_DQ, ROW_CQ, ROW_CIQ, ROW_CV = 0, 512, 1024, 1536, 1792
T_ROWS = 1856

CHUNK = 256


def _alibi_slopes():
    s = np.exp2(-8.0 * np.arange(1, 25, dtype=np.float32) / 24).astype(np.float32)
    return [float(v) for v in s[0::3]], [float(v) for v in s[1::3]], [float(v) for v in s[2::3]]


SLOPES_A, SLOPES_B, SLOPES_C = _alibi_slopes()


def _params(semantics, vmem_mib):
    return pltpu.CompilerParams(dimension_semantics=semantics, vmem_limit_bytes=vmem_mib * 2**20)


def _resident(shape):
    nd = len(shape)
    return pl.BlockSpec(shape, lambda *_: (0,) * nd, pipeline_mode=pl.Buffered(1))


def _nt_dot(a, b):
    return lax.dot_general(a, b, (((1,), (1,)), ((), ())), preferred_element_type=F32)


def _lane_lo(rows):
    return lax.broadcasted_iota(I32, (rows, 128), 1) < HEAD_DIM


def _split3(v):
    hi = v.astype(BF16)
    r1 = v - hi.astype(F32)
    mid = r1.astype(BF16)
    lo = (r1 - mid.astype(F32)).astype(BF16)
    return hi, mid, lo


def _layer_norm(z, g, b):
    mu = jnp.mean(z, axis=-1, keepdims=True)
    zc = z - mu
    var = jnp.mean(zc * zc, axis=-1, keepdims=True)
    return zc * lax.rsqrt(var + LN_EPS) * g + b


PROJ_TM = 512
PROJ_CHUNK = 256


B_SLABS = 3 * BRANCH_WIDTH // 128


def _proj_kernel(x_ref, w_ref, cs_ref, wm_ref, wt_ref, ts_ref, qkv_ref, misc_ref, t_ref, c4_ref, c16_ref,
                 b_scr):
    xb = x_ref[...].astype(BF16)
    for c in range(PROJ_WIDTH // PROJ_CHUNK):
        sl = slice(c * PROJ_CHUNK, (c + 1) * PROJ_CHUNK)
        acc = jnp.dot(xb, w_ref[:, sl], preferred_element_type=F32) * cs_ref[:, sl]
        qkv_ref[:, sl] = acc.astype(BF16)
        for half in range(PROJ_CHUNK // 128):
            col = c * PROJ_CHUNK + half * 128
            if COL_BQ <= col < COL_BV + BRANCH_WIDTH:
                b_scr[(col - COL_BQ) // 128] = acc[:, half * 128:(half + 1) * 128]
    for dil, ref in ((4, c4_ref), (16, c16_ref)):
        n = PROJ_TM // dil
        for r in range(dil):
            for s in range(B_SLABS):
                ref[r, :, s * 128:(s + 1) * 128] = b_scr[s, pl.ds(r, n, stride=dil), :].astype(BF16)
    misc_ref[...] = _nt_dot(wm_ref[...], xb)
    for c in range(PROJ_TM // CHUNK):
        acc = _nt_dot(wt_ref[...], xb[c * CHUNK:(c + 1) * CHUNK, :])
        t_ref[c] = (acc * ts_ref[...]).astype(BF16)


def _project(x2d, wcat, cscale, wmisc, wt, tscale, batch, seq):
    m = x2d.shape[0]
    tps = seq // PROJ_TM
    cls_spec = lambda dil: pl.BlockSpec(
        (None, dil, PROJ_TM // dil, 3 * BRANCH_WIDTH), lambda i: (i // tps, 0, i % tps, 0))
    cls_shape = lambda dil: jax.ShapeDtypeStruct((batch, dil, seq // dil, 3 * BRANCH_WIDTH), BF16)
    return pl.pallas_call(
        _proj_kernel,
        grid=(m // PROJ_TM,),
        in_specs=[
            pl.BlockSpec((PROJ_TM, D_MODEL), lambda i: (i, 0)),
            _resident((D_MODEL, PROJ_WIDTH)),
            _resident((1, PROJ_WIDTH)),
            _resident((MISC_ROWS, D_MODEL)),
            _resident((T_ROWS, D_MODEL)),
            _resident((T_ROWS, 1)),
        ],
        out_specs=[
            pl.BlockSpec((PROJ_TM, PROJ_WIDTH), lambda i: (i, 0)),
            pl.BlockSpec((MISC_ROWS, PROJ_TM), lambda i: (0, i)),
            pl.BlockSpec((PROJ_TM // CHUNK, T_ROWS, CHUNK), lambda i: (i, 0, 0)),
            cls_spec(4),
            cls_spec(16),
        ],
        out_shape=[
            jax.ShapeDtypeStruct((m, PROJ_WIDTH), BF16),
            jax.ShapeDtypeStruct((MISC_ROWS, m), F32),
            jax.ShapeDtypeStruct((m // CHUNK, T_ROWS, CHUNK), BF16),
            cls_shape(4),
            cls_shape(16),
        ],
        scratch_shapes=[pltpu.VMEM((B_SLABS, PROJ_TM, 128), F32)],
        compiler_params=_params(("parallel",), 48),
        name="in_proj",
    )(x2d, wcat, cscale, wmisc, wt, tscale)


KX_ONES = 24


def _cum_kernel(misc_ref, bf_ref, c_ref, kx_ref, *, seq):
    f = misc_ref[8:16, :] + bf_ref[...]
    ls = -(jnp.maximum(-f, 0.0) + jnp.log1p(jnp.exp(-jnp.abs(f))))
    row = lax.broadcasted_iota(I32, (128, 128), 0)
    col = lax.broadcasted_iota(I32, (128, 128), 1)
    upper = jnp.where(row <= col, 1.0, 0.0).astype(BF16)
    carry = jnp.zeros((8, 1), F32)
    ones = jnp.ones((8, 128), F32)
    pad = jnp.zeros((128 - 32, 128), F32)
    for j in range(seq // 128):
        hi, mid, lo = _split3(ls[:, j * 128:(j + 1) * 128])
        cs = (jnp.dot(hi, upper, preferred_element_type=F32)
              + jnp.dot(mid, upper, preferred_element_type=F32)
              + jnp.dot(lo, upper, preferred_element_type=F32))
        c = cs + carry
        c_ref[:, j * 128:(j + 1) * 128] = c
        carry = carry + cs[:, 127:128]
        nhi, nmid, nlo = _split3(-(c * LOG2E))
        stage = jnp.concatenate([nhi.astype(F32), nmid.astype(F32), nlo.astype(F32), ones, pad], axis=0)
        kx_ref[j * 128:(j + 1) * 128, :] = stage.T.astype(BF16)


def _cum_forget(misc, b_forget, batch, seq):
    return pl.pallas_call(
        functools.partial(_cum_kernel, seq=seq),
        grid=(batch,),
        in_specs=[
            pl.BlockSpec((MISC_ROWS, seq), lambda b: (0, b)),
            pl.BlockSpec((8, 1), lambda b: (0, 0)),
        ],
        out_specs=[
            pl.BlockSpec((8, seq), lambda b: (0, b)),
            pl.BlockSpec((seq, 128), lambda b: (b, 0)),
        ],
        out_shape=[
            jax.ShapeDtypeStruct((8, batch * seq), F32),
            jax.ShapeDtypeStruct((batch * seq, 128), BF16),
        ],
        compiler_params=_params(("parallel",), 32),
        name="forget_cumsum",
    )(misc, b_forget.reshape(8, 1))


def _np_split3(v):
    out, rest = [], np.asarray(v, np.float32)
    for _ in range(3):
        bits = rest.view(np.uint32)
        term = ((bits + 0x7FFF + ((bits >> 16) & 1)) & 0xFFFF0000).astype(np.uint32).view(np.float32)
        out.append(term)
        rest = (rest - term).astype(np.float32)
    return out


def _band_bias_features(slopes, dist_scale, head_order):
    qx = np.zeros((N_HEADS, BLOCK, 128), np.float32)
    for pos, h in enumerate(head_order):
        sl = np.float32(slopes[h]) * np.float32(dist_scale) * LOG2E
        for i, term in enumerate(_np_split3(np.full((BLOCK,), sl, np.float32))):
            qx[pos, :, i] = term
        for i, term in enumerate(_np_split3(-sl * (np.arange(BLOCK, dtype=np.float32) + BLOCK))):
            qx[pos, :, 3 + i] = term
    kx = np.zeros((2 * BLOCK, 128), np.float32)
    kx[:, 0:3] = np.arange(2 * BLOCK, dtype=np.float32)[:, None]
    kx[:, 3:6] = 1.0
    return jnp.asarray(qx, BF16), jnp.asarray(kx, BF16)


def _band_window(max_dist):
    qi = np.arange(2 * BLOCK)[:, None] % BLOCK
    ki = np.arange(2 * BLOCK)[None, :]
    dist = qi - ki + BLOCK
    inside = (dist >= 0) & (dist <= max_dist)
    return jnp.asarray(np.stack([inside & (ki >= BLOCK), inside]).astype(np.float32))


def _band_kernel(*refs, kv_width, has_sinks, head_order, want_lw):
    refs = list(refs)
    sink_ref = refs.pop(0) if has_sinks else None
    q_ref, kp_ref, kc_ref, vp_ref, vc_ref, qx_ref, kx_ref, win_ref = refs[:8]
    o_ref = refs[8]
    lw_ref = refs[9] if want_lw else None

    valid = win_ref[jnp.minimum(pl.program_id(1), 1)] > 0.5
    lo = _lane_lo(BLOCK)
    first_head = lax.broadcasted_iota(I32, (2 * BLOCK, 1), 0) < BLOCK
    kx = kx_ref[...]

    for p in range(4):
        q2 = q_ref[:, p * 128:(p + 1) * 128]
        zeros = jnp.zeros_like(q2)
        lhs = jnp.concatenate(
            [jnp.concatenate([jnp.where(lo, q2, zeros), qx_ref[2 * p]], axis=1),
             jnp.concatenate([jnp.where(lo, zeros, q2), qx_ref[2 * p + 1]], axis=1)], axis=0)
        ksl = slice(0, 128) if kv_width == 128 else slice(p * 128, (p + 1) * 128)
        k2 = jnp.concatenate([kp_ref[:, ksl], kc_ref[:, ksl]], axis=0)
        v2 = jnp.concatenate([vp_ref[:, ksl], vc_ref[:, ksl]], axis=0)
        s = _nt_dot(lhs, jnp.concatenate([k2, kx], axis=1))
        s = jnp.where(valid, s, NEG_INF)
        m = jnp.max(s, axis=1, keepdims=True)
        if has_sinks:
            sk = jnp.where(first_head, sink_ref[head_order[2 * p]], sink_ref[head_order[2 * p + 1]]) * LOG2E
            m = jnp.maximum(m, sk)
        pexp = jnp.exp2(s - m)
        l = jnp.sum(pexp, axis=1, keepdims=True)
        if has_sinks:
            l = l + jnp.exp2(sk - m)
        o2 = jnp.dot(pexp.astype(BF16), v2, preferred_element_type=F32) / l
        o_ref[:, p * 128:(p + 1) * 128] = jnp.where(lo, o2[0:BLOCK], o2[BLOCK:2 * BLOCK]).astype(o_ref.dtype)
        if want_lw:
            lw = jnp.broadcast_to(m + jnp.log2(l), (2 * BLOCK, 128))
            lw_ref[:, p * 128:(p + 1) * 128] = jnp.where(lo, lw[0:BLOCK], lw[BLOCK:2 * BLOCK])


def _banded_attention(arr, qcol, kcol, vcol, kv_width, *, slopes, dist_scale, max_dist,
                      sinks=None, head_order=PLAIN_ORDER, want_lw=False, out_dtype=F32, name):
    batch, length, _ = arr.shape
    nb = length // BLOCK
    kern = functools.partial(
        _band_kernel, kv_width=kv_width,
        has_sinks=sinks is not None, head_order=head_order, want_lw=want_lw)
    qx, kx = _band_bias_features(slopes, dist_scale, head_order)
    window = _band_window(max_dist)
    prev = lambda i: jnp.maximum(i - 1, 0)
    in_specs = [
        pl.BlockSpec((None, BLOCK, BRANCH_WIDTH), lambda b, i: (b, i, qcol)),
        pl.BlockSpec((None, BLOCK, kv_width), lambda b, i: (b, prev(i), kcol)),
        pl.BlockSpec((None, BLOCK, kv_width), lambda b, i: (b, i, kcol)),
        pl.BlockSpec((None, BLOCK, kv_width), lambda b, i: (b, prev(i), vcol)),
        pl.BlockSpec((None, BLOCK, kv_width), lambda b, i: (b, i, vcol)),
        _resident((N_HEADS, BLOCK, 128)),
        _resident((2 * BLOCK, 128)),
        _resident((2, 2 * BLOCK, 2 * BLOCK)),
    ]
    args = [arr, arr, arr, arr, arr, qx, kx, window]
    if sinks is not None:
        in_specs = [pl.BlockSpec(memory_space=pltpu.SMEM)] + in_specs
        args = [sinks] + args
    o_spec = pl.BlockSpec((None, BLOCK, BRANCH_WIDTH), lambda b, i: (b, i, 0))
    o_shape = jax.ShapeDtypeStruct((batch, length, BRANCH_WIDTH), out_dtype)
    if want_lw:
        out_specs = [o_spec, o_spec]
        out_shape = [o_shape, jax.ShapeDtypeStruct((batch, length, BRANCH_WIDTH), F32)]
    else:
        out_specs, out_shape = o_spec, o_shape
    return pl.pallas_call(
        kern, grid=(batch, nb), in_specs=in_specs, out_specs=out_specs, out_shape=out_shape,
        compiler_params=_params(("parallel", "arbitrary"), 32), name=name,
    )(*args)


def _stash_scores(s, park, h):
    s_scr, mc_scr = park
    s_scr[h] = s
    mc_scr[h] = jnp.max(s, axis=0, keepdims=True)


def _fold_scores(park, h, vt, m_scr, l_scr, acc_scr):
    s_scr, mc_scr = park
    s = s_scr[h]
    m_prev = m_scr[h]
    m_new = jnp.maximum(m_prev, mc_scr[h])
    alpha = jnp.exp2(m_prev - m_new)
    pexp = jnp.exp2(s - m_new)
    l_scr[h] = alpha * l_scr[h] + jnp.sum(pexp, axis=0, keepdims=True)
    m_scr[h] = m_new
    rows = slice(h * HEAD_DIM, (h + 1) * HEAD_DIM)
    acc_scr[rows, :] = alpha * acc_scr[rows, :] + jnp.dot(vt, pexp.astype(BF16), preferred_element_type=F32)


def _init_softmax(m_scr, l_scr, acc_scr):
    m_scr[...] = jnp.full(m_scr.shape, NEG_INF, F32)
    l_scr[...] = jnp.zeros(l_scr.shape, F32)
    acc_scr[...] = jnp.zeros(acc_scr.shape, F32)


def _write_heads(o_ref, l_scr, acc_scr):
    t = CHUNK
    for p in range(4):
        l2 = jnp.concatenate(
            [jnp.broadcast_to(l_scr[2 * p], (HEAD_DIM, t)),
             jnp.broadcast_to(l_scr[2 * p + 1], (HEAD_DIM, t))], axis=0)
        o_ref[:, p * 128:(p + 1) * 128] = (acc_scr[p * 128:(p + 1) * 128, :] / l2).T.astype(o_ref.dtype)


def _row_select(rows, width, pieces):
    ridx = lax.broadcasted_iota(I32, (rows, width), 0)
    out = jnp.zeros((rows, width), F32)
    for r, piece in pieces.items():
        out = jnp.where(ridx == r, piece, out)
    return out


def _fox_kernel(k_ref, vt_ref, qt_ref, kx_ref, crow_ref, o_ref,
                w_scr, s0_scr, mc0_scr, s1_scr, mc1_scr, s2_scr, mc2_scr, m_scr, l_scr, acc_scr):
    t = CHUNK
    qi = pl.program_id(1)
    lane = lax.broadcasted_iota(I32, (t, 128), 1)
    lo = lane < HEAD_DIM

    c_terms = [x.astype(F32) for x in _split3(crow_ref[...] * LOG2E)]
    ridx = lax.broadcasted_iota(I32, (128, t), 0)
    for p in range(4):
        pieces = {}
        for i in range(3):
            pieces[KX_ONES + i] = c_terms[i][2 * p:2 * p + 1, :]
            pieces[KX_ONES + 3 + i] = c_terms[i][2 * p + 1:2 * p + 2, :]
        ext = jnp.where(ridx < KX_ONES, 1.0, _row_select(128, t, pieces))
        w_scr[p, 0:128, :] = qt_ref[p * 128:(p + 1) * 128, :]
        w_scr[p, 128:256, :] = ext.astype(BF16)

    def bias_lanes(h, half):
        first = KX_ONES + 3 * half
        return (lane == h) | (lane == 8 + h) | (lane == 16 + h) | ((lane >= first) & (lane < first + 3))

    _init_softmax(m_scr, l_scr, acc_scr)
    causal = lax.broadcasted_iota(I32, (t, t), 0) <= lax.broadcasted_iota(I32, (t, t), 1)

    parks = ((s0_scr, mc0_scr), (s1_scr, mc1_scr), (s2_scr, mc2_scr))

    def score(j, park, diagonal=False):
        start = pl.multiple_of(j * t, t)
        kx = kx_ref[pl.ds(start, t), :]
        zeros = jnp.zeros_like(kx)
        for p in range(4):
            k2 = k_ref[pl.ds(start, t), p * 128:(p + 1) * 128]
            lhs_a = jnp.concatenate(
                [jnp.where(lo, k2, zeros), jnp.where(bias_lanes(2 * p, 0), kx, zeros)], axis=1)
            lhs_b = jnp.concatenate(
                [jnp.where(lo, zeros, k2), jnp.where(bias_lanes(2 * p + 1, 1), kx, zeros)], axis=1)
            s2 = jnp.dot(jnp.concatenate([lhs_a, lhs_b], axis=0), w_scr[p], preferred_element_type=F32)
            for half in range(2):
                s = s2[half * t:(half + 1) * t, :]
                if diagonal:
                    s = jnp.where(causal, s, NEG_INF)
                _stash_scores(s, park, 2 * p + half)

    def fold(j, park):
        vt = vt_ref[j]
        for h in range(N_HEADS):
            _fold_scores(park, h, vt[h * HEAD_DIM:(h + 1) * HEAD_DIM, :], m_scr, l_scr, acc_scr)

    score(qi, parks[2], diagonal=True)

    @pl.when(qi > 0)
    def _():
        score(0, parks[0])

    fold(qi, parks[2])

    def body(i, carry):
        score(2 * i + 1, parks[1])
        fold(2 * i, parks[0])
        score(2 * i + 2, parks[0])
        fold(2 * i + 1, parks[1])
        return carry

    pairs = jnp.maximum(qi - 1, 0) // 2
    lax.fori_loop(0, pairs, body, 0)

    @pl.when((qi > 0) & (qi % 2 == 1))
    def _():
        fold(qi - 1, parks[0])

    @pl.when((qi > 0) & (qi % 2 == 0))
    def _():
        score(qi - 1, parks[1])
        fold(qi - 2, parks[0])
        fold(qi - 1, parks[1])

    _write_heads(o_ref, l_scr, acc_scr)


def _fox_attention(qkv3, tproj, kx, c_t):
    batch, seq, _ = qkv3.shape
    t = CHUNK
    nq = seq // t
    return pl.pallas_call(
        _fox_kernel,
        grid=(batch, nq),
        in_specs=[
            pl.BlockSpec((None, seq, BRANCH_WIDTH), lambda b, i: (b, 0, COL_DK // 512)),
            pl.BlockSpec((nq, BRANCH_WIDTH, t), lambda b, i: (b, ROW_DV // 512, 0)),
            pl.BlockSpec((None, BRANCH_WIDTH, t), lambda b, i: (b * nq + i, ROW_DQ // 512, 0)),
            pl.BlockSpec((seq, 128), lambda b, i: (b, 0)),
            pl.BlockSpec((8, t), lambda b, i: (0, b * nq + i)),
        ],
        out_specs=pl.BlockSpec((None, t, BRANCH_WIDTH), lambda b, i: (b, i, 0)),
        out_shape=jax.ShapeDtypeStruct((batch, seq, BRANCH_WIDTH), BF16),
        scratch_shapes=[
            pltpu.VMEM((4, 256, t), BF16),
        ] + [pltpu.VMEM((N_HEADS, t, t), F32), pltpu.VMEM((N_HEADS, 1, t), F32)] * 3 + [
            pltpu.VMEM((N_HEADS, 1, t), F32),
            pltpu.VMEM((N_HEADS, 1, t), F32),
            pltpu.VMEM((BRANCH_WIDTH, t), F32),
        ],
        compiler_params=_params(("parallel", "arbitrary"), 48),
        name="fox_attention",
    )(qkv3, tproj, tproj, kx, c_t)


COUNT_ROWS = 128


def _bf16_terms(value):
    out, rest = [], np.float32(value)
    for _ in range(3):
        bits = np.array([rest], np.float32).view(np.uint32)
        rounded = ((bits + 0x7FFF + ((bits >> 16) & 1)) & 0xFFFF0000).astype(np.uint32)
        term = rounded.view(np.float32)[0]
        out.append(float(term))
        rest = np.float32(rest - term)
    return out


SLOPES_C_LOG2 = [float(np.float32(s) * LOG2E) for s in SLOPES_C]
SLOPE_TERMS_C = [_bf16_terms(s) for s in SLOPES_C_LOG2]


def _dsa_kernel(k_ref, vt_ref, qt_ref, iqt_ref, ik_ref, iw_ref, o_ref,
                keys_scr, w_scr, s0_scr, mc0_scr, s1_scr, mc1_scr, m_scr, l_scr, acc_scr, *, topk):
    t = CHUNK
    qi = pl.program_id(1)
    n_chunks = qi + 1
    q0 = qi * t
    lane = lax.broadcasted_iota(I32, (t, 128), 1)
    lo = lane < HEAD_DIM
    qpos_row = q0 + lax.broadcasted_iota(I32, (1, t), 1)

    iws = [iw_ref[h:h + 1, :] for h in range(IDX_HEADS)]

    def score_chunk(j, carry):
        start = pl.multiple_of(j * t, t)
        ik = ik_ref[pl.ds(start, t), :]
        zeros = jnp.zeros_like(ik)
        ik_lo, ik_hi = jnp.where(lo, ik, zeros), jnp.where(lo, zeros, ik)
        lhs = jnp.concatenate(
            [jnp.concatenate([ik_lo, zeros], axis=1), jnp.concatenate([ik_hi, zeros], axis=1),
             jnp.concatenate([zeros, ik_lo], axis=1), jnp.concatenate([zeros, ik_hi], axis=1)], axis=0)
        dots = jnp.dot(lhs, iqt_ref[...], preferred_element_type=F32)
        score = jnp.zeros((t, t), F32)
        for h in range(IDX_HEADS):
            score = score + iws[h] * jnp.maximum(dots[h * t:(h + 1) * t, :], 0.0)
        bits = lax.bitcast_convert_type(score, I32)
        key = jnp.where(bits < 0, jnp.int32(INT_MIN) - bits, bits)
        kpos = start + lax.broadcasted_iota(I32, (t, 1), 0)
        keys_scr[pl.ds(start, t), :] = jnp.where(kpos <= qpos_row, key, jnp.int32(INT_MIN))
        return carry

    lax.fori_loop(0, n_chunks, score_chunk, 0)

    n_groups = n_chunks * (t // COUNT_ROWS)

    def count(cand, strict):
        def body(g, acc):
            blk = keys_scr[pl.ds(pl.multiple_of(g * COUNT_ROWS, COUNT_ROWS), COUNT_ROWS), :]
            hit = ((blk > cand) if strict else (blk >= cand)).reshape(COUNT_ROWS // 32, 32, t)
            for i in range(COUNT_ROWS // 32):
                acc = jnp.where(hit[i], acc + 1, acc)
            return acc
        acc = lax.fori_loop(0, n_groups, body, jnp.zeros((32, t), I32))
        return jnp.sum(acc, axis=0, keepdims=True)

    def bit_round(r, thr):
        cand = thr + lax.shift_left(jnp.int32(1), 31 - r)
        return jnp.where(count(cand, False) >= topk, cand, thr)

    thr = lax.fori_loop(0, 32, bit_round, jnp.full((1, t), INT_MIN, I32))
    ties_wanted = jnp.where(thr == INT_MIN, 0, topk - count(thr, True)).astype(F32)

    tpos = qpos_row.astype(F32)
    for h in range(N_HEADS):
        p, half = divmod(h, 2)
        neg_t = _split3(-(np.float32(SLOPES_C_LOG2[h]) * tpos))
        pieces = {}
        for i in range(3):
            pieces[i] = np.float32(64.0 * SLOPE_TERMS_C[h][i])
            pieces[3 + i] = np.float32(SLOPE_TERMS_C[h][i])
            pieces[6 + i] = neg_t[i].astype(F32)
        base = half * 128
        w_scr[p, base:base + HEAD_DIM, :] = qt_ref[h * HEAD_DIM:(h + 1) * HEAD_DIM, :]
        w_scr[p, base + HEAD_DIM:base + 128, :] = _row_select(HEAD_DIM, t, pieces).astype(BF16)

    krow = lax.broadcasted_iota(I32, (t, 128), 0)
    rel = lane - HEAD_DIM

    def key_extras(start):
        a = lax.shift_right_logical(start + krow, 6).astype(F32)
        b = (krow & 63).astype(F32)
        ex = jnp.where((rel >= 0) & (rel < 3), a, 0.0)
        ex = jnp.where((rel >= 3) & (rel < 6), b, ex)
        ex = jnp.where((rel >= 6) & (rel < 9), 1.0, ex)
        return ex.astype(BF16)

    _init_softmax(m_scr, l_scr, acc_scr)
    row = lax.broadcasted_iota(I32, (t, t), 0)
    col = lax.broadcasted_iota(I32, (t, t), 1)
    lower = jnp.where(col <= row, 1.0, 0.0).astype(BF16)

    parks = ((s0_scr, mc0_scr), (s1_scr, mc1_scr))

    def score(j, park, ties_seen):
        start = pl.multiple_of(j * t, t)
        key = keys_scr[pl.ds(start, t), :]
        eq = key == thr
        eqf = jnp.where(eq, 1.0, 0.0)
        prefix = jnp.dot(lower, eqf.astype(BF16), preferred_element_type=F32)
        rank = prefix - eqf + ties_seen
        sel = (key > thr) | (eq & (rank < ties_wanted))
        k_aug = jnp.where(lo, k_ref[pl.ds(start, t), :], key_extras(start))
        zeros = jnp.zeros_like(k_aug)
        lhs = jnp.concatenate(
            [jnp.concatenate([k_aug, zeros], axis=1), jnp.concatenate([zeros, k_aug], axis=1)], axis=0)
        for p in range(4):
            s2 = jnp.dot(lhs, w_scr[p], preferred_element_type=F32)
            for half in range(2):
                s = jnp.where(sel, s2[half * t:(half + 1) * t, :], NEG_INF)
                _stash_scores(s, park, 2 * p + half)
        return ties_seen + prefix[t - 1:t, :]

    def fold(j, park):
        vt = vt_ref[j]
        for h in range(N_HEADS):
            _fold_scores(park, h, vt, m_scr, l_scr, acc_scr)

    def body(i, ties_seen):
        ties_seen = score(2 * i + 1, parks[1], ties_seen)
        fold(2 * i, parks[0])
        ties_seen = score(2 * i + 2, parks[0], ties_seen)
        fold(2 * i + 1, parks[1])
        return ties_seen

    ties_seen = lax.fori_loop(0, qi // 2, body, score(0, parks[0], jnp.zeros((1, t), F32)))

    @pl.when(qi % 2 == 0)
    def _():
        fold(qi, parks[0])

    @pl.when(qi % 2 == 1)
    def _():
        score(qi, parks[1], ties_seen)
        fold(qi - 1, parks[0])
        fold(qi, parks[1])

    _write_heads(o_ref, l_scr, acc_scr)


def _dsa_attention(qkv3, tproj, misc):
    batch, seq, _ = qkv3.shape
    t = CHUNK
    nq = seq // t
    topk = min(TOPK_MAX, seq // 4)
    return pl.pallas_call(
        functools.partial(_dsa_kernel, topk=topk),
        grid=(batch, nq),
        in_specs=[
            pl.BlockSpec((None, seq, 128), lambda b, i: (b, 0, COL_CK // 128)),
            pl.BlockSpec((nq, HEAD_DIM, t), lambda b, i: (b, ROW_CV // HEAD_DIM, 0)),
            pl.BlockSpec((None, BRANCH_WIDTH, t), lambda b, i: (b * nq + i, ROW_CQ // 512, 0)),
            pl.BlockSpec((None, 256, t), lambda b, i: (b * nq + i, ROW_CIQ // 256, 0)),
            pl.BlockSpec((None, seq, 128), lambda b, i: (b, 0, COL_CIK // 128)),
            pl.BlockSpec((8, t), lambda b, i: (0, b * nq + i)),
        ],
        out_specs=pl.BlockSpec((None, t, BRANCH_WIDTH), lambda b, i: (b, i, 0)),
        out_shape=jax.ShapeDtypeStruct((batch, seq, BRANCH_WIDTH), BF16),
        scratch_shapes=[
            pltpu.VMEM((seq, t), I32),
            pltpu.VMEM((4, 256, t), BF16),
        ] + [pltpu.VMEM((N_HEADS, t, t), F32), pltpu.VMEM((N_HEADS, 1, t), F32)] * 2 + [
            pltpu.VMEM((N_HEADS, 1, t), F32),
            pltpu.VMEM((N_HEADS, 1, t), F32),
            pltpu.VMEM((BRANCH_WIDTH, t), F32),
        ],
        compiler_params=_params(("parallel", "arbitrary"), 48),
        name="dsa_attention",
    )(qkv3, tproj, tproj, tproj, qkv3, misc)


MERGE_TM = 512


def _interleave_classes(ref, dil, scr):
    n = MERGE_TM // dil
    for r in range(dil):
        for s in range(BRANCH_WIDTH // 128):
            scr[s, pl.ds(r, n, stride=dil), :] = ref[r, :, s * 128:(s + 1) * 128]
    return jnp.concatenate([scr[s] for s in range(BRANCH_WIDTH // 128)], axis=1)


def _merge_kernel(x_ref, oa_ref, ob1_ref, ob4_ref, ob16_ref, lw1_ref, lw4_ref, lw16_ref,
                  oc_ref, od_ref, wg_ref, bg_ref, wb_ref, wo_ref, g_ref, b_ref, out_ref, cls_scr):
    x = x_ref[...]
    xb = x.astype(BF16)
    ob4 = _interleave_classes(ob4_ref, 4, cls_scr.at[0])
    ob16 = _interleave_classes(ob16_ref, 16, cls_scr.at[1])
    lw4 = _interleave_classes(lw4_ref, 4, cls_scr.at[2])
    lw16 = _interleave_classes(lw16_ref, 16, cls_scr.at[3])
    lw1 = lw1_ref[...]
    top = jnp.maximum(jnp.maximum(lw1, lw4), lw16)
    w1, w4, w16 = jnp.exp2(lw1 - top), jnp.exp2(lw4 - top), jnp.exp2(lw16 - top)
    ob = (w1 * ob1_ref[...] + w4 * ob4 + w16 * ob16) / (w1 + w4 + w16)
    branches = (oa_ref[...], ob.astype(BF16), oc_ref[...], od_ref[...])
    merged = jnp.zeros((MERGE_TM, D_MODEL), F32)
    for n in range(4):
        sl = slice(n * D_MODEL, (n + 1) * D_MODEL)
        proj = jnp.dot(branches[n], wb_ref[n], preferred_element_type=F32)
        gate = jax.nn.sigmoid(jnp.dot(xb, wg_ref[:, sl], preferred_element_type=F32) + bg_ref[:, sl])
        merged = merged + gate * proj
    y = jnp.dot(merged.astype(BF16), wo_ref[...], preferred_element_type=F32)
    out_ref[...] = _layer_norm(DEEPNORM_ALPHA * x + y, g_ref[...], b_ref[...])


def _merge(x2d, oa, ob_parts, lw_parts, oc, od, wg, bg, wb, wo, g, b, seq):
    m = x2d.shape[0]
    tps = seq // MERGE_TM
    row = lambda w: pl.BlockSpec((MERGE_TM, w), lambda i: (i, 0))
    cls = lambda dil: pl.BlockSpec(
        (None, dil, MERGE_TM // dil, BRANCH_WIDTH), lambda i: (i // tps, 0, i % tps, 0))
    groups = [row(BRANCH_WIDTH), cls(4), cls(16)]
    return pl.pallas_call(
        _merge_kernel,
        grid=(m // MERGE_TM,),
        in_specs=[row(D_MODEL), row(BRANCH_WIDTH)] + groups + groups + [row(BRANCH_WIDTH)] * 2 + [
            _resident((D_MODEL, 4 * D_MODEL)),
            _resident((1, 4 * D_MODEL)),
            _resident((4, BRANCH_WIDTH, D_MODEL)),
            _resident((D_MODEL, D_MODEL)),
            _resident((1, D_MODEL)),
            _resident((1, D_MODEL)),
        ],
        out_specs=row(D_MODEL),
        out_shape=jax.ShapeDtypeStruct((m, D_MODEL), F32),
        scratch_shapes=[pltpu.VMEM((4, BRANCH_WIDTH // 128, MERGE_TM, 128), F32)],
        compiler_params=_params(("parallel",), 56),
        name="branch_merge",
    )(x2d, oa, *ob_parts, *lw_parts, oc, od, wg, bg, wb, wo, g, b)


FFN_TM = 512
FFN_CHUNK = 256
HALO = 8


def _gelu_tanh(a):
    return 0.5 * a * (1.0 + jnp.tanh(np.float32(np.sqrt(2.0 / np.pi)) * (a + 0.044715 * (a * a * a))))


def _ffn_kernel(x_ref, wu_ref, wg_ref, cw_ref, cb_ref, wd_ref, g_ref, b_ref, out_ref,
                a_scr, tail_scr, h_scr, *, tiles_per_seq):
    tm = FFN_TM
    x = x_ref[...]
    xb = x.astype(BF16)
    seq_start = (pl.program_id(0) % tiles_per_seq) == 0

    @pl.when(pl.program_id(0) == 0)
    def _():
        tail_scr[...] = jnp.zeros(tail_scr.shape, F32)

    for c in range(D_FF // FFN_CHUNK):
        sl = slice(c * FFN_CHUNK, (c + 1) * FFN_CHUNK)
        a = jnp.dot(xb, wu_ref[:, sl], preferred_element_type=F32)
        a_scr[0:HALO, :] = jnp.where(seq_start, 0.0, tail_scr[:, sl])
        a_scr[HALO:HALO + tm, :] = a
        tail_scr[:, sl] = a[tm - HALO:tm, :]
        conv = cb_ref[:, sl] + (cw_ref[0:1, sl] * a_scr[HALO - 2:HALO - 2 + tm, :]
                                + cw_ref[1:2, sl] * a_scr[HALO - 1:HALO - 1 + tm, :]
                                + cw_ref[2:3, sl] * a)
        gate = jnp.dot(xb, wg_ref[:, sl], preferred_element_type=F32)
        h_scr[:, sl] = (_gelu_tanh(conv) * gate).astype(BF16)
    y = jnp.dot(h_scr[...], wd_ref[...], preferred_element_type=F32)
    out_ref[...] = _layer_norm(DEEPNORM_ALPHA * x + y, g_ref[...], b_ref[...])


def _ffn(x2d, wu, wg, cw, cb, wd, g, b, seq):
    m = x2d.shape[0]
    tm = FFN_TM
    return pl.pallas_call(
        functools.partial(_ffn_kernel, tiles_per_seq=seq // tm),
        grid=(m // tm,),
        in_specs=[
            pl.BlockSpec((tm, D_MODEL), lambda i: (i, 0)),
            _resident((D_MODEL, D_FF)),
            _resident((D_MODEL, D_FF)),
            _resident((3, D_FF)),
            _resident((1, D_FF)),
            _resident((D_FF, D_MODEL)),
            _resident((1, D_MODEL)),
            _resident((1, D_MODEL)),
        ],
        out_specs=pl.BlockSpec((tm, D_MODEL), lambda i: (i, 0)),
        out_shape=jax.ShapeDtypeStruct((m, D_MODEL), F32),
        scratch_shapes=[pltpu.VMEM((HALO + tm, FFN_CHUNK), F32), pltpu.VMEM((HALO, D_FF), F32),
                        pltpu.VMEM((tm, D_FF), BF16)],
        compiler_params=_params(("arbitrary",), 56),
        name="conv_glu_ffn",
    )(x2d, wu, wg, cw, cb, wd, g, b)


def _prep_in_proj(w):
    offs = np.cumsum((0,) + IN_SPLIT_SIZES)
    (aq, ak, av, bq, bk, bv, cq, ck, cv, ciq, cik, ciw, dq, dk, dv, df) = [
        w[:, offs[i]:offs[i + 1]] for i in range(len(IN_SPLIT_SIZES))]
    scale = HEAD_DIM ** -0.5
    aq = aq.reshape(D_MODEL, N_HEADS, HEAD_DIM)[:, jnp.array(A_ORDER), :].reshape(D_MODEL, BRANCH_WIDTH)
    pad64 = jnp.zeros((D_MODEL, HEAD_DIM), F32)
    wcat = jnp.concatenate([aq * scale, bq * scale, bk, bv, dk, ak, av, ck, pad64, cik, cik], axis=1).astype(BF16)
    wmisc = jnp.concatenate([ciw, jnp.zeros((D_MODEL, 4), F32), df], axis=1).T.astype(BF16)
    wt = jnp.concatenate([dv, dq * scale, cq * scale, ciq * scale, cv], axis=1).T.astype(BF16)
    tscale = np.ones((T_ROWS, 1), np.float32)
    tscale[ROW_DQ:ROW_DQ + BRANCH_WIDTH] = LOG2E
    tscale[ROW_CQ:ROW_CQ + BRANCH_WIDTH] = LOG2E
    cscale = np.ones((1, PROJ_WIDTH), np.float32)
    cscale[:, COL_AQ:COL_AQ + BRANCH_WIDTH] = LOG2E
    cscale[:, COL_BQ:COL_BQ + BRANCH_WIDTH] = LOG2E
    return wcat, jnp.asarray(cscale), wmisc, wt, jnp.asarray(tscale)


def _layer(x2d, batch, seq, w_in, b_forget, sinks, w_branch, w_gate, b_gate, w_out, ln1_g, ln1_b,
           w_up, w_ffn_gate, conv_w, conv_b, w_down, ln2_g, ln2_b):
    m = batch * seq
    wcat, cscale, wmisc, wt, tscale = _prep_in_proj(w_in)
    qkv, misc, tproj, cls4, cls16 = _project(x2d, wcat, cscale, wmisc, wt, tscale, batch, seq)
    qkv3 = qkv.reshape(batch, seq, PROJ_WIDTH)
    classes = {4: cls4, 16: cls16}

    oa = _banded_attention(
        qkv3, COL_AQ // 512, COL_AK // 128, COL_AV // 128, 128, slopes=SLOPES_A, dist_scale=1,
        max_dist=A_WINDOW - 1, sinks=sinks, head_order=A_ORDER, out_dtype=BF16, name="swa_attention")

    ob_parts, lw_parts = [], []
    for window, dil in B_PATTERNS:
        if dil == 1:
            arr, cols = qkv3, (COL_BQ // 512, COL_BK // 512, COL_BV // 512)
        else:
            arr, cols = classes[dil].reshape(batch * dil, seq // dil, 3 * BRANCH_WIDTH), (0, 1, 2)
        o, lw = _banded_attention(
            arr, *cols, 512, slopes=SLOPES_B, dist_scale=dil, max_dist=window // dil, want_lw=True,
            name=f"dilated_attention_{dil}")
        shape = (m, BRANCH_WIDTH) if dil == 1 else (batch, dil, seq // dil, BRANCH_WIDTH)
        ob_parts.append(o.reshape(shape))
        lw_parts.append(lw.reshape(shape))

    oc = _dsa_attention(qkv3, tproj, misc)

    c_t, kx = _cum_forget(misc, b_forget, batch, seq)
    od = _fox_attention(qkv3, tproj, kx, c_t)

    wb = jnp.concatenate(
        [w_branch[0].reshape(N_HEADS, HEAD_DIM, D_MODEL)[jnp.array(A_ORDER)].reshape(1, BRANCH_WIDTH, D_MODEL),
         w_branch[1:]], axis=0).astype(BF16)
    x2d = _merge(
        x2d, oa.reshape(m, BRANCH_WIDTH), ob_parts, lw_parts, oc.reshape(m, BRANCH_WIDTH),
        od.reshape(m, BRANCH_WIDTH), w_gate.astype(BF16), b_gate.reshape(1, -1), wb, w_out.astype(BF16),
        ln1_g.reshape(1, -1), ln1_b.reshape(1, -1), seq)
    return _ffn(x2d, w_up.astype(BF16), w_ffn_gate.astype(BF16), conv_w, conv_b.reshape(1, -1),
                w_down.astype(BF16), ln2_g.reshape(1, -1), ln2_b.reshape(1, -1), seq)


def kernel(x, w_in, b_forget, sinks, w_branch, w_gate, b_gate, w_out, ln1_g, ln1_b,
           w_up, w_ffn_gate, conv_w, conv_b, w_down, ln2_g, ln2_b):
    batch, seq, _ = x.shape
    x2d = x.reshape(batch * seq, D_MODEL)
    for l in range(w_in.shape[0]):
        x2d = _layer(x2d, batch, seq, w_in[l], b_forget[l], sinks[l], w_branch[l], w_gate[l], b_gate[l],
                     w_out[l], ln1_g[l], ln1_b[l], w_up[l], w_ffn_gate[l], conv_w[l], conv_b[l],
                     w_down[l], ln2_g[l], ln2_b[l])
    return x2d.reshape(batch, seq, D_MODEL)
```

```python
import functools

import numpy as np
import jax
import jax.numpy as jnp
from jax import lax
from jax.experimental import pallas as pl
from jax.experimental.pallas import tpu as pltpu

F32 = jnp.float32
BF16 = jnp.bfloat16
I32 = jnp.int32

D_MODEL = 1024
HEAD_DIM = 64
N_HEADS = 8
BRANCH_WIDTH = N_HEADS * HEAD_DIM
BLOCK = 128
NEG_INF = -1e30
INT_MIN = -2147483648
LOG2E = np.float32(1.4426950408889634)
A_WINDOW = 128
B_PATTERNS = ((128, 1), (512, 4), (2048, 16))
IDX_HEADS = 4
TOPK_MAX = 256
D_FF = 2816
LN_EPS = 1e-5
DEPTH = 2
DEEPNORM_ALPHA = (2 * DEPTH) ** 0.25
IN_SPLIT_SIZES = (512, 128, 128, 512, 512, 512, 512, 64, 64, 256, 64, 4, 512, 512, 512, 8)

A_ORDER = (0, 4, 1, 5, 2, 6, 3, 7)
PLAIN_ORDER = tuple(range(N_HEADS))

COL_AQ, COL_BQ, COL_BK, COL_BV, COL_DK = (i * 512 for i in range(5))
COL_AK, COL_AV, COL_CK, COL_CIK = 2560, 2688, 2816, 2944
PROJ_WIDTH = 3072
MISC_ROWS = 16
ROW_DV, ROW_DQ, ROW_CQ, ROW_CIQ, ROW_CV = 0, 512, 1024, 1536, 1792
T_ROWS = 1856

CHUNK = 256


def _alibi_slopes():
    s = np.exp2(-8.0 * np.arange(1, 25, dtype=np.float32) / 24).astype(np.float32)
    return [float(v) for v in s[0::3]], [float(v) for v in s[1::3]], [float(v) for v in s[2::3]]


SLOPES_A, SLOPES_B, SLOPES_C = _alibi_slopes()


def _params(semantics, vmem_mib):
    return pltpu.CompilerParams(dimension_semantics=semantics, vmem_limit_bytes=vmem_mib * 2**20)


def _resident(shape):
    nd = len(shape)
    return pl.BlockSpec(shape, lambda *_: (0,) * nd, pipeline_mode=pl.Buffered(1))


def _nt_dot(a, b):
    return lax.dot_general(a, b, (((1,), (1,)), ((), ())), preferred_element_type=F32)


def _lane_lo(rows):
    return lax.broadcasted_iota(I32, (rows, 128), 1) < HEAD_DIM


def _split3(v):
    hi = v.astype(BF16)
    r1 = v - hi.astype(F32)
    mid = r1.astype(BF16)
    lo = (r1 - mid.astype(F32)).astype(BF16)
    return hi, mid, lo


def _layer_norm(z, g, b):
    mu = jnp.mean(z, axis=-1, keepdims=True)
    zc = z - mu
    var = jnp.mean(zc * zc, axis=-1, keepdims=True)
    return zc * lax.rsqrt(var + LN_EPS) * g + b


PROJ_TM = 512
PROJ_CHUNK = 256


B_SLABS = 3 * BRANCH_WIDTH // 128


def _proj_kernel(x_ref, w_ref, cs_ref, wm_ref, wt_ref, ts_ref, qkv_ref, misc_ref, t_ref, c4_ref, c16_ref,
                 b_scr):
    xb = x_ref[...].astype(BF16)
    for c in range(PROJ_WIDTH // PROJ_CHUNK):
        sl = slice(c * PROJ_CHUNK, (c + 1) * PROJ_CHUNK)
        acc = jnp.dot(xb, w_ref[:, sl], preferred_element_type=F32) * cs_ref[:, sl]
        qkv_ref[:, sl] = acc.astype(BF16)
        for half in range(PROJ_CHUNK // 128):
            col = c * PROJ_CHUNK + half * 128
            if COL_BQ <= col < COL_BV + BRANCH_WIDTH:
                b_scr[(col - COL_BQ) // 128] = acc[:, half * 128:(half + 1) * 128]
    for dil, ref in ((4, c4_ref), (16, c16_ref)):
        n = PROJ_TM // dil
        for r in range(dil):
            for s in range(B_SLABS):
                ref[r, :, s * 128:(s + 1) * 128] = b_scr[s, pl.ds(r, n, stride=dil), :].astype(BF16)
    misc_ref[...] = _nt_dot(wm_ref[...], xb)
    for c in range(PROJ_TM // CHUNK):
        acc = _nt_dot(wt_ref[...], xb[c * CHUNK:(c + 1) * CHUNK, :])
        t_ref[c] = (acc * ts_ref[...]).astype(BF16)


def _project(x2d, wcat, cscale, wmisc, wt, tscale, batch, seq):
    m = x2d.shape[0]
    tps = seq // PROJ_TM
    cls_spec = lambda dil: pl.BlockSpec(
        (None, dil, PROJ_TM // dil, 3 * BRANCH_WIDTH), lambda i: (i // tps, 0, i % tps, 0))
    cls_shape = lambda dil: jax.ShapeDtypeStruct((batch, dil, seq // dil, 3 * BRANCH_WIDTH), BF16)
    return pl.pallas_call(
        _proj_kernel,
        grid=(m // PROJ_TM,),
        in_specs=[
            pl.BlockSpec((PROJ_TM, D_MODEL), lambda i: (i, 0)),
            _resident((D_MODEL, PROJ_WIDTH)),
            _resident((1, PROJ_WIDTH)),
            _resident((MISC_ROWS, D_MODEL)),
            _resident((T_ROWS, D_MODEL)),
            _resident((T_ROWS, 1)),
        ],
        out_specs=[
            pl.BlockSpec((PROJ_TM, PROJ_WIDTH), lambda i: (i, 0)),
            pl.BlockSpec((MISC_ROWS, PROJ_TM), lambda i: (0, i)),
            pl.BlockSpec((PROJ_TM // CHUNK, T_ROWS, CHUNK), lambda i: (i, 0, 0)),
            cls_spec(4),
            cls_spec(16),
        ],
        out_shape=[
            jax.ShapeDtypeStruct((m, PROJ_WIDTH), BF16),
            jax.ShapeDtypeStruct((MISC_ROWS, m), F32),
            jax.ShapeDtypeStruct((m // CHUNK, T_ROWS, CHUNK), BF16),
            cls_shape(4),
            cls_shape(16),
        ],
        scratch_shapes=[pltpu.VMEM((B_SLABS, PROJ_TM, 128), F32)],
        compiler_params=_params(("parallel",), 48),
        name="in_proj",
    )(x2d, wcat, cscale, wmisc, wt, tscale)


KX_ONES = 24


def _cum_kernel(misc_ref, bf_ref, c_ref, kx_ref, *, seq):
    f = misc_ref[8:16, :] + bf_ref[...]
    ls = -(jnp.maximum(-f, 0.0) + jnp.log1p(jnp.exp(-jnp.abs(f))))
    row = lax.broadcasted_iota(I32, (128, 128), 0)
    col = lax.broadcasted_iota(I32, (128, 128), 1)
    upper = jnp.where(row <= col, 1.0, 0.0).astype(BF16)
    carry = jnp.zeros((8, 1), F32)
    ones = jnp.ones((8, 128), F32)
    pad = jnp.zeros((128 - 32, 128), F32)
    for j in range(seq // 128):
        hi, mid, lo = _split3(ls[:, j * 128:(j + 1) * 128])
        cs = (jnp.dot(hi, upper, preferred_element_type=F32)
              + jnp.dot(mid, upper, preferred_element_type=F32)
              + jnp.dot(lo, upper, preferred_element_type=F32))
        c = cs + carry
        c_ref[:, j * 128:(j + 1) * 128] = c
        carry = carry + cs[:, 127:128]
        nhi, nmid, nlo = _split3(-(c * LOG2E))
        stage = jnp.concatenate([nhi.astype(F32), nmid.astype(F32), nlo.astype(F32), ones, pad], axis=0)
        kx_ref[j * 128:(j + 1) * 128, :] = stage.T.astype(BF16)


def _cum_forget(misc, b_forget, batch, seq):
    return pl.pallas_call(
        functools.partial(_cum_kernel, seq=seq),
        grid=(batch,),
        in_specs=[
            pl.BlockSpec((MISC_ROWS, seq), lambda b: (0, b)),
            pl.BlockSpec((8, 1), lambda b: (0, 0)),
        ],
        out_specs=[
            pl.BlockSpec((8, seq), lambda b: (0, b)),
            pl.BlockSpec((seq, 128), lambda b: (b, 0)),
        ],
        out_shape=[
            jax.ShapeDtypeStruct((8, batch * seq), F32),
            jax.ShapeDtypeStruct((batch * seq, 128), BF16),
        ],
        compiler_params=_params(("parallel",), 32),
        name="forget_cumsum",
    )(misc, b_forget.reshape(8, 1))


def _np_split3(v):
    out, rest = [], np.asarray(v, np.float32)
    for _ in range(3):
        bits = rest.view(np.uint32)
        term = ((bits + 0x7FFF + ((bits >> 16) & 1)) & 0xFFFF0000).astype(np.uint32).view(np.float32)
        out.append(term)
        rest = (rest - term).astype(np.float32)
    return out


def _band_bias_features(slopes, dist_scale, head_order):
    qx = np.zeros((N_HEADS, BLOCK, 128), np.float32)
    for pos, h in enumerate(head_order):
        sl = np.float32(slopes[h]) * np.float32(dist_scale) * LOG2E
        for i, term in enumerate(_np_split3(np.full((BLOCK,), sl, np.float32))):
            qx[pos, :, i] = term
        for i, term in enumerate(_np_split3(-sl * (np.arange(BLOCK, dtype=np.float32) + BLOCK))):
            qx[pos, :, 3 + i] = term
    kx = np.zeros((2 * BLOCK, 128), np.float32)
    kx[:, 0:3] = np.arange(2 * BLOCK, dtype=np.float32)[:, None]
    kx[:, 3:6] = 1.0
    return jnp.asarray(qx, BF16), jnp.asarray(kx, BF16)


def _band_window(max_dist):
    qi = np.arange(2 * BLOCK)[:, None] % BLOCK
    ki = np.arange(2 * BLOCK)[None, :]
    dist = qi - ki + BLOCK
    inside = (dist >= 0) & (dist <= max_dist)
    return jnp.asarray(np.stack([inside & (ki >= BLOCK), inside]).astype(np.float32))


def _band_kernel(*refs, kv_width, has_sinks, head_order, want_lw):
    refs = list(refs)
    sink_ref = refs.pop(0) if has_sinks else None
    q_ref, kp_ref, kc_ref, vp_ref, vc_ref, qx_ref, kx_ref, win_ref = refs[:8]
    o_ref = refs[8]
    lw_ref = refs[9] if want_lw else None

    valid = win_ref[jnp.minimum(pl.program_id(1), 1)] > 0.5
    lo = _lane_lo(BLOCK)
    first_head = lax.broadcasted_iota(I32, (2 * BLOCK, 1), 0) < BLOCK
    kx = kx_ref[...]

    for p in range(4):
        q2 = q_ref[:, p * 128:(p + 1) * 128]
        zeros = jnp.zeros_like(q2)
        lhs = jnp.concatenate(
            [jnp.concatenate([jnp.where(lo, q2, zeros), qx_ref[2 * p]], axis=1),
             jnp.concatenate([jnp.where(lo, zeros, q2), qx_ref[2 * p + 1]], axis=1)], axis=0)
        ksl = slice(0, 128) if kv_width == 128 else slice(p * 128, (p + 1) * 128)
        k2 = jnp.concatenate([kp_ref[:, ksl], kc_ref[:, ksl]], axis=0)
        v2 = jnp.concatenate([vp_ref[:, ksl], vc_ref[:, ksl]], axis=0)
        s = _nt_dot(lhs, jnp.concatenate([k2, kx], axis=1))
        s = jnp.where(valid, s, NEG_INF)
        m = jnp.max(s, axis=1, keepdims=True)
        if has_sinks:
            sk = jnp.where(first_head, sink_ref[head_order[2 * p]], sink_ref[head_order[2 * p + 1]]) * LOG2E
            m = jnp.maximum(m, sk)
        pexp = jnp.exp2(s - m)
        l = jnp.sum(pexp, axis=1, keepdims=True)
        if has_sinks:
            l = l + jnp.exp2(sk - m)
        o2 = jnp.dot(pexp.astype(BF16), v2, preferred_element_type=F32) / l
        o_ref[:, p * 128:(p + 1) * 128] = jnp.where(lo, o2[0:BLOCK], o2[BLOCK:2 * BLOCK]).astype(o_ref.dtype)
        if want_lw:
            lw = jnp.broadcast_to(m + jnp.log2(l), (2 * BLOCK, 128))
            lw_ref[:, p * 128:(p + 1) * 128] = jnp.where(lo, lw[0:BLOCK], lw[BLOCK:2 * BLOCK])


def _banded_attention(arr, qcol, kcol, vcol, kv_width, *, slopes, dist_scale, max_dist,
                      sinks=None, head_order=PLAIN_ORDER, want_lw=False, out_dtype=F32, name):
    batch, length, _ = arr.shape
    nb = length // BLOCK
    kern = functools.partial(
        _band_kernel, kv_width=kv_width,
        has_sinks=sinks is not None, head_order=head_order, want_lw=want_lw)
    qx, kx = _band_bias_features(slopes, dist_scale, head_order)
    window = _band_window(max_dist)
    prev = lambda i: jnp.maximum(i - 1, 0)
    in_specs = [
        pl.BlockSpec((None, BLOCK, BRANCH_WIDTH), lambda b, i: (b, i, qcol)),
        pl.BlockSpec((None, BLOCK, kv_width), lambda b, i: (b, prev(i), kcol)),
        pl.BlockSpec((None, BLOCK, kv_width), lambda b, i: (b, i, kcol)),
        pl.BlockSpec((None, BLOCK, kv_width), lambda b, i: (b, prev(i), vcol)),
        pl.BlockSpec((None, BLOCK, kv_width), lambda b, i: (b, i, vcol)),
        _resident((N_HEADS, BLOCK, 128)),
        _resident((2 * BLOCK, 128)),
        _resident((2, 2 * BLOCK, 2 * BLOCK)),
    ]
    args = [arr, arr, arr, arr, arr, qx, kx, window]
    if sinks is not None:
        in_specs = [pl.BlockSpec(memory_space=pltpu.SMEM)] + in_specs
        args = [sinks] + args
    o_spec = pl.BlockSpec((None, BLOCK, BRANCH_WIDTH), lambda b, i: (b, i, 0))
    o_shape = jax.ShapeDtypeStruct((batch, length, BRANCH_WIDTH), out_dtype)
    if want_lw:
        out_specs = [o_spec, o_spec]
        out_shape = [o_shape, jax.ShapeDtypeStruct((batch, length, BRANCH_WIDTH), F32)]
    else:
        out_specs, out_shape = o_spec, o_shape
    return pl.pallas_call(
        kern, grid=(batch, nb), in_specs=in_specs, out_specs=out_specs, out_shape=out_shape,
        compiler_params=_params(("parallel", "arbitrary"), 32), name=name,
    )(*args)


def _stash_scores(s, park, h):
    s_scr, mc_scr = park
    s_scr[h] = s
    mc_scr[h] = jnp.max(s, axis=0, keepdims=True)


def _fold_scores(park, h, vt, m_scr, l_scr, acc_scr):
    s_scr, mc_scr = park
    s = s_scr[h]
    m_prev = m_scr[h]
    m_new = jnp.maximum(m_prev, mc_scr[h])
    alpha = jnp.exp2(m_prev - m_new)
    pexp = jnp.exp2(s - m_new)
    l_scr[h] = alpha * l_scr[h] + jnp.sum(pexp, axis=0, keepdims=True)
    m_scr[h] = m_new
    rows = slice(h * HEAD_DIM, (h + 1) * HEAD_DIM)
    acc_scr[rows, :] = alpha * acc_scr[rows, :] + jnp.dot(vt, pexp.astype(BF16), preferred_element_type=F32)


def _init_softmax(m_scr, l_scr, acc_scr):
    m_scr[...] = jnp.full(m_scr.shape, NEG_INF, F32)
    l_scr[...] = jnp.zeros(l_scr.shape, F32)
    acc_scr[...] = jnp.zeros(acc_scr.shape, F32)


def _write_heads(o_ref, l_scr, acc_scr):
    t = CHUNK
    for p in range(4):
        l2 = jnp.concatenate(
            [jnp.broadcast_to(l_scr[2 * p], (HEAD_DIM, t)),
             jnp.broadcast_to(l_scr[2 * p + 1], (HEAD_DIM, t))], axis=0)
        o_ref[:, p * 128:(p + 1) * 128] = (acc_scr[p * 128:(p + 1) * 128, :] / l2).T.astype(o_ref.dtype)


def _row_select(rows, width, pieces):
    ridx = lax.broadcasted_iota(I32, (rows, width), 0)
    out = jnp.zeros((rows, width), F32)
    for r, piece in pieces.items():
        out = jnp.where(ridx == r, piece, out)
    return out


def _fox_kernel(k_ref, vt_ref, qt_ref, kx_ref, crow_ref, o_ref,
                w_scr, s0_scr, mc0_scr, s1_scr, mc1_scr, s2_scr, mc2_scr, m_scr, l_scr, acc_scr):
    t = CHUNK
    qi = pl.program_id(1)
    lane = lax.broadcasted_iota(I32, (t, 128), 1)
    lo = lane < HEAD_DIM

    c_terms = [x.astype(F32) for x in _split3(crow_ref[...] * LOG2E)]
    ridx = lax.broadcasted_iota(I32, (128, t), 0)
    for p in range(4):
        pieces = {}
        for i in range(3):
            pieces[KX_ONES + i] = c_terms[i][2 * p:2 * p + 1, :]
            pieces[KX_ONES + 3 + i] = c_terms[i][2 * p + 1:2 * p + 2, :]
        ext = jnp.where(ridx < KX_ONES, 1.0, _row_select(128, t, pieces))
        w_scr[p, 0:128, :] = qt_ref[p * 128:(p + 1) * 128, :]
        w_scr[p, 128:256, :] = ext.astype(BF16)

    def bias_lanes(h, half):
        first = KX_ONES + 3 * half
        return (lane == h) | (lane == 8 + h) | (lane == 16 + h) | ((lane >= first) & (lane < first + 3))

    _init_softmax(m_scr, l_scr, acc_scr)
    causal = lax.broadcasted_iota(I32, (t, t), 0) <= lax.broadcasted_iota(I32, (t, t), 1)

    parks = ((s0_scr, mc0_scr), (s1_scr, mc1_scr), (s2_scr, mc2_scr))

    def score(j, park, diagonal=False):
        start = pl.multiple_of(j * t, t)
        kx = kx_ref[pl.ds(start, t), :]
        zeros = jnp.zeros_like(kx)
        for p in range(4):
            k2 = k_ref[pl.ds(start, t), p * 128:(p + 1) * 128]
            lhs_a = jnp.concatenate(
                [jnp.where(lo, k2, zeros), jnp.where(bias_lanes(2 * p, 0), kx, zeros)], axis=1)
            lhs_b = jnp.concatenate(
                [jnp.where(lo, zeros, k2), jnp.where(bias_lanes(2 * p + 1, 1), kx, zeros)], axis=1)
            s2 = jnp.dot(jnp.concatenate([lhs_a, lhs_b], axis=0), w_scr[p], preferred_element_type=F32)
            for half in range(2):
                s = s2[half * t:(half + 1) * t, :]
                if diagonal:
                    s = jnp.where(causal, s, NEG_INF)
                _stash_scores(s, park, 2 * p + half)

    def fold(j, park):
        vt = vt_ref[j]
        for h in range(N_HEADS):
            _fold_scores(park, h, vt[h * HEAD_DIM:(h + 1) * HEAD_DIM, :], m_scr, l_scr, acc_scr)

    score(qi, parks[2], diagonal=True)

    @pl.when(qi > 0)
    def _():
        score(0, parks[0])

    fold(qi, parks[2])

    def body(i, carry):
        score(2 * i + 1, parks[1])
        fold(2 * i, parks[0])
        score(2 * i + 2, parks[0])
        fold(2 * i + 1, parks[1])
        return carry

    pairs = jnp.maximum(qi - 1, 0) // 2
    lax.fori_loop(0, pairs, body, 0)

    @pl.when((qi > 0) & (qi % 2 == 1))
    def _():
        fold(qi - 1, parks[0])

    @pl.when((qi > 0) & (qi % 2 == 0))
    def _():
        score(qi - 1, parks[1])
        fold(qi - 2, parks[0])
        fold(qi - 1, parks[1])

    _write_heads(o_ref, l_scr, acc_scr)


def _fox_attention(qkv3, tproj, kx, c_t):
    batch, seq, _ = qkv3.shape
    t = CHUNK
    nq = seq // t
    return pl.pallas_call(
        _fox_kernel,
        grid=(batch, nq),
        in_specs=[
            pl.BlockSpec((None, seq, BRANCH_WIDTH), lambda b, i: (b, 0, COL_DK // 512)),
            pl.BlockSpec((nq, BRANCH_WIDTH, t), lambda b, i: (b, ROW_DV // 512, 0)),
            pl.BlockSpec((None, BRANCH_WIDTH, t), lambda b, i: (b * nq + i, ROW_DQ // 512, 0)),
            pl.BlockSpec((seq, 128), lambda b, i: (b, 0)),
            pl.BlockSpec((8, t), lambda b, i: (0, b * nq + i)),
        ],
        out_specs=pl.BlockSpec((None, t, BRANCH_WIDTH), lambda b, i: (b, i, 0)),
        out_shape=jax.ShapeDtypeStruct((batch, seq, BRANCH_WIDTH), BF16),
        scratch_shapes=[
            pltpu.VMEM((4, 256, t), BF16),
        ] + [pltpu.VMEM((N_HEADS, t, t), F32), pltpu.VMEM((N_HEADS, 1, t), F32)] * 3 + [
            pltpu.VMEM((N_HEADS, 1, t), F32),
            pltpu.VMEM((N_HEADS, 1, t), F32),
            pltpu.VMEM((BRANCH_WIDTH, t), F32),
        ],
        compiler_params=_params(("parallel", "arbitrary"), 48),
        name="fox_attention",
    )(qkv3, tproj, tproj, kx, c_t)


COUNT_ROWS = 32
I16 = jnp.int16
I16_MIN = -32768


def _bf16_terms(value):
    out, rest = [], np.float32(value)
    for _ in range(3):
        bits = np.array([rest], np.float32).view(np.uint32)
        rounded = ((bits + 0x7FFF + ((bits >> 16) & 1)) & 0xFFFF0000).astype(np.uint32)
        term = rounded.view(np.float32)[0]
        out.append(float(term))
        rest = np.float32(rest - term)
    return out


SLOPES_C_LOG2 = [float(np.float32(s) * LOG2E) for s in SLOPES_C]
SLOPE_TERMS_C = [_bf16_terms(s) for s in SLOPES_C_LOG2]


def _dsa_kernel(k_ref, vt_ref, qt_ref, iqt_ref, ik_ref, iw_ref, o_ref,
                keys_scr, half_scr, w_scr, s0_scr, mc0_scr, s1_scr, mc1_scr, m_scr, l_scr, acc_scr, *, topk):
    t = CHUNK
    qi = pl.program_id(1)
    n_chunks = qi + 1
    q0 = qi * t
    lane = lax.broadcasted_iota(I32, (t, 128), 1)
    lo = lane < HEAD_DIM
    qpos_row = q0 + lax.broadcasted_iota(I32, (1, t), 1)

    iws = [iw_ref[h:h + 1, :] for h in range(IDX_HEADS)]

    def score_chunk(j, carry):
        start = pl.multiple_of(j * t, t)
        ik = ik_ref[pl.ds(start, t), :]
        zeros = jnp.zeros_like(ik)
        ik_lo, ik_hi = jnp.where(lo, ik, zeros), jnp.where(lo, zeros, ik)
        lhs = jnp.concatenate(
            [jnp.concatenate([ik_lo, zeros], axis=1), jnp.concatenate([ik_hi, zeros], axis=1),
             jnp.concatenate([zeros, ik_lo], axis=1), jnp.concatenate([zeros, ik_hi], axis=1)], axis=0)
        dots = jnp.dot(lhs, iqt_ref[...], preferred_element_type=F32)
        score = jnp.zeros((t, t), F32)
        for h in range(IDX_HEADS):
            score = score + iws[h] * jnp.maximum(dots[h * t:(h + 1) * t, :], 0.0)
        bits = lax.bitcast_convert_type(score, I32)
        key = jnp.where(bits < 0, jnp.int32(INT_MIN) - bits, bits)
        kpos = start + lax.broadcasted_iota(I32, (t, 1), 0)
        key = jnp.where(kpos <= qpos_row, key, jnp.int32(INT_MIN))
        keys_scr[pl.ds(start, t), :] = key
        half_scr[pl.ds(start, t), :] = lax.shift_right_arithmetic(key, 16).astype(I16)
        return carry

    lax.fori_loop(0, n_chunks, score_chunk, 0)

    def count16(cand, strict):
        cand16 = cand.astype(I16)

        def body(j, acc):
            blk = half_scr[pl.ds(pl.multiple_of(j * t, t), t), :]
            hit = ((blk > cand16) if strict else (blk >= cand16)).reshape(t // COUNT_ROWS, COUNT_ROWS, t)
            for i in range(t // COUNT_ROWS):
                acc = jnp.where(hit[i], acc + jnp.int16(1), acc)
            return acc

        acc = lax.fori_loop(0, n_chunks, body, jnp.zeros((COUNT_ROWS, t), I16))
        return jnp.sum(acc.astype(I32), axis=0, keepdims=True)

    def bisect16(wanted):
        def bit_round(r, best):
            cand = best + lax.shift_left(jnp.int32(1), 15 - r)
            return jnp.where(count16(cand, False) >= wanted, cand, best)
        return lax.fori_loop(0, 16, bit_round, jnp.full((1, t), I16_MIN, I32))

    hi = bisect16(topk)
    above = count16(hi, True)

    def low_chunk(j, carry):
        start = pl.multiple_of(j * t, t)
        key = keys_scr[pl.ds(start, t), :]
        low = jnp.where(lax.shift_right_arithmetic(key, 16) == hi, key ^ 0x8000, 0x8000)
        half_scr[pl.ds(start, t), :] = low.astype(I16)
        return carry

    lax.fori_loop(0, n_chunks, low_chunk, 0)
    low = bisect16(topk - above)
    thr = lax.shift_left(hi, 16) + ((low + 0x8000) & 0xFFFF)
    ties_wanted = jnp.where(thr == INT_MIN, 0, topk - above - count16(low, True)).astype(F32)

    tpos = qpos_row.astype(F32)
    for h in range(N_HEADS):
        p, half = divmod(h, 2)
        neg_t = _split3(-(np.float32(SLOPES_C_LOG2[h]) * tpos))
        pieces = {}
        for i in range(3):
            pieces[i] = np.float32(64.0 * SLOPE_TERMS_C[h][i])
            pieces[3 + i] = np.float32(SLOPE_TERMS_C[h][i])
            pieces[6 + i] = neg_t[i].astype(F32)
        base = half * 128
        w_scr[p, base:base + HEAD_DIM, :] = qt_ref[h * HEAD_DIM:(h + 1) * HEAD_DIM, :]
        w_scr[p, base + HEAD_DIM:base + 128, :] = _row_select(HEAD_DIM, t, pieces).astype(BF16)

    krow = lax.broadcasted_iota(I32, (t, 128), 0)
    rel = lane - HEAD_DIM

    def key_extras(start):
        a = lax.shift_right_logical(start + krow, 6).astype(F32)
        b = (krow & 63).astype(F32)
        ex = jnp.where((rel >= 0) & (rel < 3), a, 0.0)
        ex = jnp.where((rel >= 3) & (rel < 6), b, ex)
        ex = jnp.where((rel >= 6) & (rel < 9), 1.0, ex)
        return ex.astype(BF16)

    _init_softmax(m_scr, l_scr, acc_scr)
    row = lax.broadcasted_iota(I32, (t, t), 0)
    col = lax.broadcasted_iota(I32, (t, t), 1)
    lower = jnp.where(col <= row, 1.0, 0.0).astype(BF16)

    parks = ((s0_scr, mc0_scr), (s1_scr, mc1_scr))

    def score(j, park, ties_seen):
        start = pl.multiple_of(j * t, t)
        key = keys_scr[pl.ds(start, t), :]
        eq = key == thr
        eqf = jnp.where(eq, 1.0, 0.0)
        prefix = jnp.dot(lower, eqf.astype(BF16), preferred_element_type=F32)
        rank = prefix - eqf + ties_seen
        sel = (key > thr) | (eq & (rank < ties_wanted))
        k_aug = jnp.where(lo, k_ref[pl.ds(start, t), :], key_extras(start))
        zeros = jnp.zeros_like(k_aug)
        lhs = jnp.concatenate(
            [jnp.concatenate([k_aug, zeros], axis=1), jnp.concatenate([zeros, k_aug], axis=1)], axis=0)
        for p in range(4):
            s2 = jnp.dot(lhs, w_scr[p], preferred_element_type=F32)
            for half in range(2):
                s = jnp.where(sel, s2[half * t:(half + 1) * t, :], NEG_INF)
                _stash_scores(s, park, 2 * p + half)
        return ties_seen + prefix[t - 1:t, :]

    def fold(j, park):
        vt = vt_ref[j]
        for h in range(N_HEADS):
            _fold_scores(park, h, vt, m_scr, l_scr, acc_scr)

    def body(i, ties_seen):
        ties_seen = score(2 * i + 1, parks[1], ties_seen)
        fold(2 * i, parks[0])
        ties_seen = score(2 * i + 2, parks[0], ties_seen)
        fold(2 * i + 1, parks[1])
        return ties_seen

    ties_seen = lax.fori_loop(0, qi // 2, body, score(0, parks[0], jnp.zeros((1, t), F32)))

    @pl.when(qi % 2 == 0)
    def _():
        fold(qi, parks[0])

    @pl.when(qi % 2 == 1)
    def _():
        score(qi, parks[1], ties_seen)
        fold(qi - 1, parks[0])
        fold(qi, parks[1])

    _write_heads(o_ref, l_scr, acc_scr)


def _dsa_attention(qkv3, tproj, misc):
    batch, seq, _ = qkv3.shape
    t = CHUNK
    nq = seq // t
    topk = min(TOPK_MAX, seq // 4)
    return pl.pallas_call(
        functools.partial(_dsa_kernel, topk=topk),
        grid=(batch, nq),
        in_specs=[
            pl.BlockSpec((None, seq, 128), lambda b, i: (b, 0, COL_CK // 128)),
            pl.BlockSpec((nq, HEAD_DIM, t), lambda b, i: (b, ROW_CV // HEAD_DIM, 0)),
            pl.BlockSpec((None, BRANCH_WIDTH, t), lambda b, i: (b * nq + i, ROW_CQ // 512, 0)),
            pl.BlockSpec((None, 256, t), lambda b, i: (b * nq + i, ROW_CIQ // 256, 0)),
            pl.BlockSpec((None, seq, 128), lambda b, i: (b, 0, COL_CIK // 128)),
            pl.BlockSpec((8, t), lambda b, i: (0, b * nq + i)),
        ],
        out_specs=pl.BlockSpec((None, t, BRANCH_WIDTH), lambda b, i: (b, i, 0)),
        out_shape=jax.ShapeDtypeStruct((batch, seq, BRANCH_WIDTH), BF16),
        scratch_shapes=[
            pltpu.VMEM((seq, t), I32),
            pltpu.VMEM((seq, t), I16),
            pltpu.VMEM((4, 256, t), BF16),
        ] + [pltpu.VMEM((N_HEADS, t, t), F32), pltpu.VMEM((N_HEADS, 1, t), F32)] * 2 + [
            pltpu.VMEM((N_HEADS, 1, t), F32),
            pltpu.VMEM((N_HEADS, 1, t), F32),
            pltpu.VMEM((BRANCH_WIDTH, t), F32),
        ],
        compiler_params=_params(("parallel", "arbitrary"), 48),
        name="dsa_attention",
    )(qkv3, tproj, tproj, tproj, qkv3, misc)


MERGE_TM = 512


def _interleave_classes(ref, dil, scr):
    n = MERGE_TM // dil
    for r in range(dil):
        for s in range(BRANCH_WIDTH // 128):
            scr[s, pl.ds(r, n, stride=dil), :] = ref[r, :, s * 128:(s + 1) * 128]
    return jnp.concatenate([scr[s] for s in range(BRANCH_WIDTH // 128)], axis=1)


def _merge_kernel(x_ref, oa_ref, ob1_ref, ob4_ref, ob16_ref, lw1_ref, lw4_ref, lw16_ref,
                  oc_ref, od_ref, wg_ref, bg_ref, wb_ref, wo_ref, g_ref, b_ref, out_ref, cls_scr):
    x = x_ref[...]
    xb = x.astype(BF16)
    ob4 = _interleave_classes(ob4_ref, 4, cls_scr.at[0])
    ob16 = _interleave_classes(ob16_ref, 16, cls_scr.at[1])
    lw4 = _interleave_classes(lw4_ref, 4, cls_scr.at[2])
    lw16 = _interleave_classes(lw16_ref, 16, cls_scr.at[3])
    lw1 = lw1_ref[...]
    top = jnp.maximum(jnp.maximum(lw1, lw4), lw16)
    w1, w4, w16 = jnp.exp2(lw1 - top), jnp.exp2(lw4 - top), jnp.exp2(lw16 - top)
    ob = (w1 * ob1_ref[...] + w4 * ob4 + w16 * ob16) / (w1 + w4 + w16)
    branches = (oa_ref[...], ob.astype(BF16), oc_ref[...], od_ref[...])
    merged = jnp.zeros((MERGE_TM, D_MODEL), F32)
    for n in range(4):
        sl = slice(n * D_MODEL, (n + 1) * D_MODEL)
        proj = jnp.dot(branches[n], wb_ref[n], preferred_element_type=F32)
        gate = jax.nn.sigmoid(jnp.dot(xb, wg_ref[:, sl], preferred_element_type=F32) + bg_ref[:, sl])
        merged = merged + gate * proj
    y = jnp.dot(merged.astype(BF16), wo_ref[...], preferred_element_type=F32)
    out_ref[...] = _layer_norm(DEEPNORM_ALPHA * x + y, g_ref[...], b_ref[...])


def _merge(x2d, oa, ob_parts, lw_parts, oc, od, wg, bg, wb, wo, g, b, seq):
    m = x2d.shape[0]
    tps = seq // MERGE_TM
    row = lambda w: pl.BlockSpec((MERGE_TM, w), lambda i: (i, 0))
    cls = lambda dil: pl.BlockSpec(
        (None, dil, MERGE_TM // dil, BRANCH_WIDTH), lambda i: (i // tps, 0, i % tps, 0))
    groups = [row(BRANCH_WIDTH), cls(4), cls(16)]
    return pl.pallas_call(
        _merge_kernel,
        grid=(m // MERGE_TM,),
        in_specs=[row(D_MODEL), row(BRANCH_WIDTH)] + groups + groups + [row(BRANCH_WIDTH)] * 2 + [
            _resident((D_MODEL, 4 * D_MODEL)),
            _resident((1, 4 * D_MODEL)),
            _resident((4, BRANCH_WIDTH, D_MODEL)),
            _resident((D_MODEL, D_MODEL)),
            _resident((1, D_MODEL)),
            _resident((1, D_MODEL)),
        ],
        out_specs=row(D_MODEL),
        out_shape=jax.ShapeDtypeStruct((m, D_MODEL), F32),
        scratch_shapes=[pltpu.VMEM((4, BRANCH_WIDTH // 128, MERGE_TM, 128), F32)],
        compiler_params=_params(("parallel",), 56),
        name="branch_merge",
    )(x2d, oa, *ob_parts, *lw_parts, oc, od, wg, bg, wb, wo, g, b)


FFN_TM = 512
FFN_CHUNK = 256
HALO = 8


def _gelu_tanh(a):
    return 0.5 * a * (1.0 + jnp.tanh(np.float32(np.sqrt(2.0 / np.pi)) * (a + 0.044715 * (a * a * a))))


def _ffn_kernel(x_ref, wu_ref, wg_ref, cw_ref, cb_ref, wd_ref, g_ref, b_ref, out_ref,
                a_scr, tail_scr, h_scr, *, tiles_per_seq):
    tm = FFN_TM
    x = x_ref[...]
    xb = x.astype(BF16)
    seq_start = (pl.program_id(0) % tiles_per_seq) == 0

    @pl.when(pl.program_id(0) == 0)
    def _():
        tail_scr[...] = jnp.zeros(tail_scr.shape, F32)

    for c in range(D_FF // FFN_CHUNK):
        sl = slice(c * FFN_CHUNK, (c + 1) * FFN_CHUNK)
        a = jnp.dot(xb, wu_ref[:, sl], preferred_element_type=F32)
        a_scr[0:HALO, :] = jnp.where(seq_start, 0.0, tail_scr[:, sl])
        a_scr[HALO:HALO + tm, :] = a
        tail_scr[:, sl] = a[tm - HALO:tm, :]
        conv = cb_ref[:, sl] + (cw_ref[0:1, sl] * a_scr[HALO - 2:HALO - 2 + tm, :]
                                + cw_ref[1:2, sl] * a_scr[HALO - 1:HALO - 1 + tm, :]
                                + cw_ref[2:3, sl] * a)
        gate = jnp.dot(xb, wg_ref[:, sl], preferred_element_type=F32)
        h_scr[:, sl] = (_gelu_tanh(conv) * gate).astype(BF16)
    y = jnp.dot(h_scr[...], wd_ref[...], preferred_element_type=F32)
    out_ref[...] = _layer_norm(DEEPNORM_ALPHA * x + y, g_ref[...], b_ref[...])


def _ffn(x2d, wu, wg, cw, cb, wd, g, b, seq):
    m = x2d.shape[0]
    tm = FFN_TM
    return pl.pallas_call(
        functools.partial(_ffn_kernel, tiles_per_seq=seq // tm),
        grid=(m // tm,),
        in_specs=[
            pl.BlockSpec((tm, D_MODEL), lambda i: (i, 0)),
            _resident((D_MODEL, D_FF)),
            _resident((D_MODEL, D_FF)),
            _resident((3, D_FF)),
            _resident((1, D_FF)),
            _resident((D_FF, D_MODEL)),
            _resident((1, D_MODEL)),
            _resident((1, D_MODEL)),
        ],
        out_specs=pl.BlockSpec((tm, D_MODEL), lambda i: (i, 0)),
        out_shape=jax.ShapeDtypeStruct((m, D_MODEL), F32),
        scratch_shapes=[pltpu.VMEM((HALO + tm, FFN_CHUNK), F32), pltpu.VMEM((HALO, D_FF), F32),
                        pltpu.VMEM((tm, D_FF), BF16)],
        compiler_params=_params(("arbitrary",), 56),
        name="conv_glu_ffn",
    )(x2d, wu, wg, cw, cb, wd, g, b)


def _prep_in_proj(w):
    offs = np.cumsum((0,) + IN_SPLIT_SIZES)
    (aq, ak, av, bq, bk, bv, cq, ck, cv, ciq, cik, ciw, dq, dk, dv, df) = [
        w[:, offs[i]:offs[i + 1]] for i in range(len(IN_SPLIT_SIZES))]
    scale = HEAD_DIM ** -0.5
    aq = aq.reshape(D_MODEL, N_HEADS, HEAD_DIM)[:, jnp.array(A_ORDER), :].reshape(D_MODEL, BRANCH_WIDTH)
    pad64 = jnp.zeros((D_MODEL, HEAD_DIM), F32)
    wcat = jnp.concatenate([aq * scale, bq * scale, bk, bv, dk, ak, av, ck, pad64, cik, cik], axis=1).astype(BF16)
    wmisc = jnp.concatenate([ciw, jnp.zeros((D_MODEL, 4), F32), df], axis=1).T.astype(BF16)
    wt = jnp.concatenate([dv, dq * scale, cq * scale, ciq * scale, cv], axis=1).T.astype(BF16)
    tscale = np.ones((T_ROWS, 1), np.float32)
    tscale[ROW_DQ:ROW_DQ + BRANCH_WIDTH] = LOG2E
    tscale[ROW_CQ:ROW_CQ + BRANCH_WIDTH] = LOG2E
    cscale = np.ones((1, PROJ_WIDTH), np.float32)
    cscale[:, COL_AQ:COL_AQ + BRANCH_WIDTH] = LOG2E
    cscale[:, COL_BQ:COL_BQ + BRANCH_WIDTH] = LOG2E
    return wcat, jnp.asarray(cscale), wmisc, wt, jnp.asarray(tscale)


def _layer(x2d, batch, seq, w_in, b_forget, sinks, w_branch, w_gate, b_gate, w_out, ln1_g, ln1_b,
           w_up, w_ffn_gate, conv_w, conv_b, w_down, ln2_g, ln2_b):
    m = batch * seq
    wcat, cscale, wmisc, wt, tscale = _prep_in_proj(w_in)
    qkv, misc, tproj, cls4, cls16 = _project(x2d, wcat, cscale, wmisc, wt, tscale, batch, seq)
    qkv3 = qkv.reshape(batch, seq, PROJ_WIDTH)
    classes = {4: cls4, 16: cls16}

    oa = _banded_attention(
        qkv3, COL_AQ // 512, COL_AK // 128, COL_AV // 128, 128, slopes=SLOPES_A, dist_scale=1,
        max_dist=A_WINDOW - 1, sinks=sinks, head_order=A_ORDER, out_dtype=BF16, name="swa_attention")

    ob_parts, lw_parts = [], []
    for window, dil in B_PATTERNS:
        if dil == 1:
            arr, cols = qkv3, (COL_BQ // 512, COL_BK // 512, COL_BV // 512)
        else:
            arr, cols = classes[dil].reshape(batch * dil, seq // dil, 3 * BRANCH_WIDTH), (0, 1, 2)
        o, lw = _banded_attention(
            arr, *cols, 512, slopes=SLOPES_B, dist_scale=dil, max_dist=window // dil, want_lw=True,
            name=f"dilated_attention_{dil}")
        shape = (m, BRANCH_WIDTH) if dil == 1 else (batch, dil, seq // dil, BRANCH_WIDTH)
        ob_parts.append(o.reshape(shape))
        lw_parts.append(lw.reshape(shape))

    oc = _dsa_attention(qkv3, tproj, misc)

    c_t, kx = _cum_forget(misc, b_forget, batch, seq)
    od = _fox_attention(qkv3, tproj, kx, c_t)

    wb = jnp.concatenate(
        [w_branch[0].reshape(N_HEADS, HEAD_DIM, D_MODEL)[jnp.array(A_ORDER)].reshape(1, BRANCH_WIDTH, D_MODEL),
         w_branch[1:]], axis=0).astype(BF16)
    x2d = _merge(
        x2d, oa.reshape(m, BRANCH_WIDTH), ob_parts, lw_parts, oc.reshape(m, BRANCH_WIDTH),
        od.reshape(m, BRANCH_WIDTH), w_gate.astype(BF16), b_gate.reshape(1, -1), wb, w_out.astype(BF16),
        ln1_g.reshape(1, -1), ln1_b.reshape(1, -1), seq)
    return _ffn(x2d, w_up.astype(BF16), w_ffn_gate.astype(BF16), conv_w, conv_b.reshape(1, -1),
                w_down.astype(BF16), ln2_g.reshape(1, -1), ln2_b.reshape(1, -1), seq)


def kernel(x, w_in, b_forget, sinks, w_branch, w_gate, b_gate, w_out, ln1_g, ln1_b,
           w_up, w_ffn_gate, conv_w, conv_b, w_down, ln2_g, ln2_b):
    batch, seq, _ = x.shape
    x2d = x.reshape(batch * seq, D_MODEL)
    for l in range(w_in.shape[0]):
        x2d = _layer(x2d, batch, seq, w_in[l], b_forget[l], sinks[l], w_branch[l], w_gate[l], b_gate[l],
                     w_out[l], ln1_g[l], ln1_b[l], w_up[l], w_ffn_gate[l], conv_w[l], conv_b[l],
                     w_down[l], ln2_g[l], ln2_b[l])
    return x2d.reshape(batch, seq, D_MODEL)
```

```python
import functools

import numpy as np
import jax
import jax.numpy as jnp
from jax import lax
from jax.experimental import pallas as pl
from jax.experimental.pallas import tpu as pltpu

F32 = jnp.float32
BF16 = jnp.bfloat16
I32 = jnp.int32

D_MODEL = 1024
HEAD_DIM = 64
N_HEADS = 8
BRANCH_WIDTH = N_HEADS * HEAD_DIM
BLOCK = 128
NEG_INF = -1e30
INT_MIN = -2147483648
LOG2E = np.float32(1.4426950408889634)
A_WINDOW = 128
B_PATTERNS = ((128, 1), (512, 4), (2048, 16))
IDX_HEADS = 4
TOPK_MAX = 256
D_FF = 2816
LN_EPS = 1e-5
DEPTH = 2
DEEPNORM_ALPHA = (2 * DEPTH) ** 0.25
IN_SPLIT_SIZES = (512, 128, 128, 512, 512, 512, 512, 64, 64, 256, 64, 4, 512, 512, 512, 8)

A_ORDER = (0, 4, 1, 5, 2, 6, 3, 7)
PLAIN_ORDER = tuple(range(N_HEADS))

COL_AQ, COL_BQ, COL_BK, COL_BV, COL_DK = (i * 512 for i in range(5))
COL_AK, COL_AV, COL_CK, COL_CIK = 2560, 2688, 2816, 2944
PROJ_WIDTH = 3072
MISC_ROWS = 16
ROW_DV, ROW_DQ, ROW_CQ, ROW_CIQ, ROW_CV = 0, 512, 1024, 1536, 1792
T_ROWS = 1856

CHUNK = 256


def _alibi_slopes():
    s = np.exp2(-8.0 * np.arange(1, 25, dtype=np.float32) / 24).astype(np.float32)
    return [float(v) for v in s[0::3]], [float(v) for v in s[1::3]], [float(v) for v in s[2::3]]


SLOPES_A, SLOPES_B, SLOPES_C = _alibi_slopes()


def _params(semantics, vmem_mib):
    return pltpu.CompilerParams(dimension_semantics=semantics, vmem_limit_bytes=vmem_mib * 2**20)


def _resident(shape):
    nd = len(shape)
    return pl.BlockSpec(shape, lambda *_: (0,) * nd, pipeline_mode=pl.Buffered(1))


def _nt_dot(a, b):
    return lax.dot_general(a, b, (((1,), (1,)), ((), ())), preferred_element_type=F32)


def _lane_lo(rows):
    return lax.broadcasted_iota(I32, (rows, 128), 1) < HEAD_DIM


def _split3(v):
    hi = v.astype(BF16)
    r1 = v - hi.astype(F32)
    mid = r1.astype(BF16)
    lo = (r1 - mid.astype(F32)).astype(BF16)
    return hi, mid, lo


def _layer_norm(z, g, b):
    mu = jnp.mean(z, axis=-1, keepdims=True)
    zc = z - mu
    var = jnp.mean(zc * zc, axis=-1, keepdims=True)
    return zc * lax.rsqrt(var + LN_EPS) * g + b


PROJ_TM = 512
PROJ_CHUNK = 256


B_SLABS = 3 * BRANCH_WIDTH // 128


def _proj_kernel(x_ref, w_ref, cs_ref, wm_ref, wt_ref, ts_ref, qkv_ref, misc_ref, t_ref, c4_ref, c16_ref,
                 b_scr):
    xb = x_ref[...].astype(BF16)
    for c in range(PROJ_WIDTH // PROJ_CHUNK):
        sl = slice(c * PROJ_CHUNK, (c + 1) * PROJ_CHUNK)
        acc = jnp.dot(xb, w_ref[:, sl], preferred_element_type=F32) * cs_ref[:, sl]
        qkv_ref[:, sl] = acc.astype(BF16)
        for half in range(PROJ_CHUNK // 128):
            col = c * PROJ_CHUNK + half * 128
            if COL_BQ <= col < COL_BV + BRANCH_WIDTH:
                b_scr[(col - COL_BQ) // 128] = acc[:, half * 128:(half + 1) * 128]
    for dil, ref in ((4, c4_ref), (16, c16_ref)):
        n = PROJ_TM // dil
        for r in range(dil):
            for s in range(B_SLABS):
                ref[r, :, s * 128:(s + 1) * 128] = b_scr[s, pl.ds(r, n, stride=dil), :].astype(BF16)
    misc_ref[...] = _nt_dot(wm_ref[...], xb)
    for c in range(PROJ_TM // CHUNK):
        acc = _nt_dot(wt_ref[...], xb[c * CHUNK:(c + 1) * CHUNK, :])
        t_ref[c] = (acc * ts_ref[...]).astype(BF16)


def _project(x2d, wcat, cscale, wmisc, wt, tscale, batch, seq):
    m = x2d.shape[0]
    tps = seq // PROJ_TM
    cls_spec = lambda dil: pl.BlockSpec(
        (None, dil, PROJ_TM // dil, 3 * BRANCH_WIDTH), lambda i: (i // tps, 0, i % tps, 0))
    cls_shape = lambda dil: jax.ShapeDtypeStruct((batch, dil, seq // dil, 3 * BRANCH_WIDTH), BF16)
    return pl.pallas_call(
        _proj_kernel,
        grid=(m // PROJ_TM,),
        in_specs=[
            pl.BlockSpec((PROJ_TM, D_MODEL), lambda i: (i, 0)),
            _resident((D_MODEL, PROJ_WIDTH)),
            _resident((1, PROJ_WIDTH)),
            _resident((MISC_ROWS, D_MODEL)),
            _resident((T_ROWS, D_MODEL)),
            _resident((T_ROWS, 1)),
        ],
        out_specs=[
            pl.BlockSpec((PROJ_TM, PROJ_WIDTH), lambda i: (i, 0)),
            pl.BlockSpec((MISC_ROWS, PROJ_TM), lambda i: (0, i)),
            pl.BlockSpec((PROJ_TM // CHUNK, T_ROWS, CHUNK), lambda i: (i, 0, 0)),
            cls_spec(4),
            cls_spec(16),
        ],
        out_shape=[
            jax.ShapeDtypeStruct((m, PROJ_WIDTH), BF16),
            jax.ShapeDtypeStruct((MISC_ROWS, m), F32),
            jax.ShapeDtypeStruct((m // CHUNK, T_ROWS, CHUNK), BF16),
            cls_shape(4),
            cls_shape(16),
        ],
        scratch_shapes=[pltpu.VMEM((B_SLABS, PROJ_TM, 128), F32)],
        compiler_params=_params(("parallel",), 48),
        name="in_proj",
    )(x2d, wcat, cscale, wmisc, wt, tscale)


KX_ONES = 24


def _cum_kernel(misc_ref, bf_ref, c_ref, kx_ref, *, seq):
    f = misc_ref[8:16, :] + bf_ref[...]
    ls = -(jnp.maximum(-f, 0.0) + jnp.log1p(jnp.exp(-jnp.abs(f))))
    row = lax.broadcasted_iota(I32, (128, 128), 0)
    col = lax.broadcasted_iota(I32, (128, 128), 1)
    upper = jnp.where(row <= col, 1.0, 0.0).astype(BF16)
    carry = jnp.zeros((8, 1), F32)
    ones = jnp.ones((8, 128), F32)
    pad = jnp.zeros((128 - 32, 128), F32)
    for j in range(seq // 128):
        hi, mid, lo = _split3(ls[:, j * 128:(j + 1) * 128])
        cs = (jnp.dot(hi, upper, preferred_element_type=F32)
              + jnp.dot(mid, upper, preferred_element_type=F32)
              + jnp.dot(lo, upper, preferred_element_type=F32))
        c = cs + carry
        c_ref[:, j * 128:(j + 1) * 128] = c
        carry = carry + cs[:, 127:128]
        nhi, nmid, nlo = _split3(-(c * LOG2E))
        stage = jnp.concatenate([nhi.astype(F32), nmid.astype(F32), nlo.astype(F32), ones, pad], axis=0)
        kx_ref[j * 128:(j + 1) * 128, :] = stage.T.astype(BF16)


def _cum_forget(misc, b_forget, batch, seq):
    return pl.pallas_call(
        functools.partial(_cum_kernel, seq=seq),
        grid=(batch,),
        in_specs=[
            pl.BlockSpec((MISC_ROWS, seq), lambda b: (0, b)),
            pl.BlockSpec((8, 1), lambda b: (0, 0)),
        ],
        out_specs=[
            pl.BlockSpec((8, seq), lambda b: (0, b)),
            pl.BlockSpec((seq, 128), lambda b: (b, 0)),
        ],
        out_shape=[
            jax.ShapeDtypeStruct((8, batch * seq), F32),
            jax.ShapeDtypeStruct((batch * seq, 128), BF16),
        ],
        compiler_params=_params(("parallel",), 32),
        name="forget_cumsum",
    )(misc, b_forget.reshape(8, 1))


BAND_BLOCKS = 4


def _np_split3(v):
    out, rest = [], np.asarray(v, np.float32)
    for _ in range(3):
        bits = rest.view(np.uint32)
        term = ((bits + 0x7FFF + ((bits >> 16) & 1)) & 0xFFFF0000).astype(np.uint32).view(np.float32)
        out.append(term)
        rest = (rest - term).astype(np.float32)
    return out


def _band_bias_features(slopes, dist_scale, head_order):
    qx = np.zeros((N_HEADS, BLOCK, 128), np.float32)
    for pos, h in enumerate(head_order):
        sl = np.float32(slopes[h]) * np.float32(dist_scale) * LOG2E
        for i, term in enumerate(_np_split3(np.full((BLOCK,), sl, np.float32))):
            qx[pos, :, i] = term
        for i, term in enumerate(_np_split3(-sl * (np.arange(BLOCK, dtype=np.float32) + BLOCK))):
            qx[pos, :, 3 + i] = term
    kx = np.zeros((2 * BLOCK, 128), np.float32)
    kx[:, 0:3] = np.arange(2 * BLOCK, dtype=np.float32)[:, None]
    kx[:, 3:6] = 1.0
    return jnp.asarray(qx, BF16), jnp.asarray(kx, BF16)


def _band_window(max_dist):
    qi = np.arange(2 * BLOCK)[:, None] % BLOCK
    ki = np.arange(2 * BLOCK)[None, :]
    dist = qi - ki + BLOCK
    inside = (dist >= 0) & (dist <= max_dist)
    return jnp.asarray(np.stack([inside & (ki >= BLOCK), inside]).astype(np.float32))


def _band_kernel(*refs, kv_width, n_blocks, has_sinks, head_order, want_lw):
    refs = list(refs)
    sink_ref = refs.pop(0) if has_sinks else None
    q_ref, kp_ref, kc_ref, vp_ref, vc_ref, qx_ref, kx_ref, win_ref = refs[:8]
    o_ref = refs[8]
    lw_ref = refs[9] if want_lw else None

    lo = _lane_lo(BLOCK)
    first_head = lax.broadcasted_iota(I32, (2 * BLOCK, 1), 0) < BLOCK
    kx = kx_ref[...]

    for blk in range(n_blocks):
        rows = slice(blk * BLOCK, (blk + 1) * BLOCK)
        if blk == 0:
            valid = win_ref[jnp.minimum(pl.program_id(1), 1)] > 0.5
        else:
            valid = win_ref[1] > 0.5
        for p in range(4):
            q2 = q_ref[rows, p * 128:(p + 1) * 128]
            zeros = jnp.zeros_like(q2)
            lhs = jnp.concatenate(
                [jnp.concatenate([jnp.where(lo, q2, zeros), qx_ref[2 * p]], axis=1),
                 jnp.concatenate([jnp.where(lo, zeros, q2), qx_ref[2 * p + 1]], axis=1)], axis=0)
            ksl = slice(0, 128) if kv_width == 128 else slice(p * 128, (p + 1) * 128)
            if blk == 0:
                k2 = jnp.concatenate([kp_ref[:, ksl], kc_ref[rows, ksl]], axis=0)
                v2 = jnp.concatenate([vp_ref[:, ksl], vc_ref[rows, ksl]], axis=0)
            else:
                k2 = kc_ref[(blk - 1) * BLOCK:(blk + 1) * BLOCK, ksl]
                v2 = vc_ref[(blk - 1) * BLOCK:(blk + 1) * BLOCK, ksl]
            s = _nt_dot(lhs, jnp.concatenate([k2, kx], axis=1))
            s = jnp.where(valid, s, NEG_INF)
            m = jnp.max(s, axis=1, keepdims=True)
            if has_sinks:
                sk = jnp.where(first_head, sink_ref[head_order[2 * p]], sink_ref[head_order[2 * p + 1]]) * LOG2E
                m = jnp.maximum(m, sk)
            pexp = jnp.exp2(s - m)
            l = jnp.sum(pexp, axis=1, keepdims=True)
            if has_sinks:
                l = l + jnp.exp2(sk - m)
            o2 = jnp.dot(pexp.astype(BF16), v2, preferred_element_type=F32) / l
            o_ref[rows, p * 128:(p + 1) * 128] = jnp.where(lo, o2[0:BLOCK], o2[BLOCK:2 * BLOCK]).astype(o_ref.dtype)
            if want_lw:
                lw = jnp.broadcast_to(m + jnp.log2(l), (2 * BLOCK, 128))
                lw_ref[rows, p * 128:(p + 1) * 128] = jnp.where(lo, lw[0:BLOCK], lw[BLOCK:2 * BLOCK])


def _banded_attention(arr, qcol, kcol, vcol, kv_width, *, slopes, dist_scale, max_dist,
                      sinks=None, head_order=PLAIN_ORDER, want_lw=False, out_dtype=F32, name):
    batch, length, _ = arr.shape
    n_blocks = min(BAND_BLOCKS, length // BLOCK)
    step = n_blocks * BLOCK
    kern = functools.partial(
        _band_kernel, kv_width=kv_width, n_blocks=n_blocks,
        has_sinks=sinks is not None, head_order=head_order, want_lw=want_lw)
    qx, kx = _band_bias_features(slopes, dist_scale, head_order)
    window = _band_window(max_dist)
    prev = lambda i: jnp.maximum(i * n_blocks - 1, 0)
    in_specs = [
        pl.BlockSpec((None, step, BRANCH_WIDTH), lambda b, i: (b, i, qcol)),
        pl.BlockSpec((None, BLOCK, kv_width), lambda b, i: (b, prev(i), kcol)),
        pl.BlockSpec((None, step, kv_width), lambda b, i: (b, i, kcol)),
        pl.BlockSpec((None, BLOCK, kv_width), lambda b, i: (b, prev(i), vcol)),
        pl.BlockSpec((None, step, kv_width), lambda b, i: (b, i, vcol)),
        _resident((N_HEADS, BLOCK, 128)),
        _resident((2 * BLOCK, 128)),
        _resident((2, 2 * BLOCK, 2 * BLOCK)),
    ]
    args = [arr, arr, arr, arr, arr, qx, kx, window]
    if sinks is not None:
        in_specs = [pl.BlockSpec(memory_space=pltpu.SMEM)] + in_specs
        args = [sinks] + args
    o_spec = pl.BlockSpec((None, step, BRANCH_WIDTH), lambda b, i: (b, i, 0))
    o_shape = jax.ShapeDtypeStruct((batch, length, BRANCH_WIDTH), out_dtype)
    if want_lw:
        out_specs = [o_spec, o_spec]
        out_shape = [o_shape, jax.ShapeDtypeStruct((batch, length, BRANCH_WIDTH), F32)]
    else:
        out_specs, out_shape = o_spec, o_shape
    return pl.pallas_call(
        kern, grid=(batch, length // step), in_specs=in_specs, out_specs=out_specs, out_shape=out_shape,
        compiler_params=_params(("parallel", "arbitrary"), 32), name=name,
    )(*args)


def _stash_scores(s, park, h):
    s_scr, mc_scr = park
    s_scr[h] = s
    mc_scr[h] = jnp.max(s, axis=0, keepdims=True)


def _fold_scores(park, h, vt, m_scr, l_scr, acc_scr):
    s_scr, mc_scr = park
    s = s_scr[h]
    m_prev = m_scr[h]
    m_new = jnp.maximum(m_prev, mc_scr[h])
    alpha = jnp.exp2(m_prev - m_new)
    pexp = jnp.exp2(s - m_new)
    l_scr[h] = alpha * l_scr[h] + jnp.sum(pexp, axis=0, keepdims=True)
    m_scr[h] = m_new
    rows = slice(h * HEAD_DIM, (h + 1) * HEAD_DIM)
    acc_scr[rows, :] = alpha * acc_scr[rows, :] + jnp.dot(vt, pexp.astype(BF16), preferred_element_type=F32)


def _init_softmax(m_scr, l_scr, acc_scr):
    m_scr[...] = jnp.full(m_scr.shape, NEG_INF, F32)
    l_scr[...] = jnp.zeros(l_scr.shape, F32)
    acc_scr[...] = jnp.zeros(acc_scr.shape, F32)


def _write_heads(o_ref, l_scr, acc_scr):
    t = CHUNK
    for p in range(4):
        l2 = jnp.concatenate(
            [jnp.broadcast_to(l_scr[2 * p], (HEAD_DIM, t)),
             jnp.broadcast_to(l_scr[2 * p + 1], (HEAD_DIM, t))], axis=0)
        o_ref[:, p * 128:(p + 1) * 128] = (acc_scr[p * 128:(p + 1) * 128, :] / l2).T.astype(o_ref.dtype)


def _row_select(rows, width, pieces):
    ridx = lax.broadcasted_iota(I32, (rows, width), 0)
    out = jnp.zeros((rows, width), F32)
    for r, piece in pieces.items():
        out = jnp.where(ridx == r, piece, out)
    return out


def _fox_kernel(k_ref, vt_ref, qt_ref, kx_ref, crow_ref, o_ref,
                w_scr, lhs_scr, s0_scr, mc0_scr, s1_scr, mc1_scr, s2_scr, mc2_scr, m_scr, l_scr, acc_scr):
    t = CHUNK
    qi = pl.program_id(1)
    lane = lax.broadcasted_iota(I32, (t, 128), 1)
    lo = lane < HEAD_DIM

    c_terms = [x.astype(F32) for x in _split3(crow_ref[...] * LOG2E)]
    ridx = lax.broadcasted_iota(I32, (128, t), 0)
    for p in range(4):
        pieces = {}
        for i in range(3):
            pieces[KX_ONES + i] = c_terms[i][2 * p:2 * p + 1, :]
            pieces[KX_ONES + 3 + i] = c_terms[i][2 * p + 1:2 * p + 2, :]
        ext = jnp.where(ridx < KX_ONES, 1.0, _row_select(128, t, pieces))
        w_scr[p, 0:128, :] = qt_ref[p * 128:(p + 1) * 128, :]
        w_scr[p, 128:256, :] = ext.astype(BF16)

    def bias_lanes(h, half):
        first = KX_ONES + 3 * half
        return (lane == h) | (lane == 8 + h) | (lane == 16 + h) | ((lane >= first) & (lane < first + 3))

    _init_softmax(m_scr, l_scr, acc_scr)
    causal = lax.broadcasted_iota(I32, (t, t), 0) <= lax.broadcasted_iota(I32, (t, t), 1)

    parks = ((s0_scr, mc0_scr), (s1_scr, mc1_scr), (s2_scr, mc2_scr))

    @pl.when(qi == 0)
    def _():
        def build(j, carry):
            start = pl.multiple_of(j * t, t)
            kx = kx_ref[pl.ds(start, t), :]
            zeros = jnp.zeros_like(kx)
            for p in range(4):
                k2 = k_ref[pl.ds(start, t), p * 128:(p + 1) * 128]
                lhs_scr[j, p, 0:t, 0:128] = jnp.where(lo, k2, zeros)
                lhs_scr[j, p, 0:t, 128:256] = jnp.where(bias_lanes(2 * p, 0), kx, zeros)
                lhs_scr[j, p, t:2 * t, 0:128] = jnp.where(lo, zeros, k2)
                lhs_scr[j, p, t:2 * t, 128:256] = jnp.where(bias_lanes(2 * p + 1, 1), kx, zeros)
            return carry

        lax.fori_loop(0, lhs_scr.shape[0], build, 0)

    def score(j, park, diagonal=False):
        for p in range(4):
            s2 = jnp.dot(lhs_scr[j, p], w_scr[p], preferred_element_type=F32)
            for half in range(2):
                s = s2[half * t:(half + 1) * t, :]
                if diagonal:
                    s = jnp.where(causal, s, NEG_INF)
                _stash_scores(s, park, 2 * p + half)

    def fold(j, park):
        vt = vt_ref[j]
        for h in range(N_HEADS):
            _fold_scores(park, h, vt[h * HEAD_DIM:(h + 1) * HEAD_DIM, :], m_scr, l_scr, acc_scr)

    score(qi, parks[2], diagonal=True)

    @pl.when(qi > 0)
    def _():
        score(0, parks[0])

    fold(qi, parks[2])

    def body(i, carry):
        score(2 * i + 1, parks[1])
        fold(2 * i, parks[0])
        score(2 * i + 2, parks[0])
        fold(2 * i + 1, parks[1])
        return carry

    pairs = jnp.maximum(qi - 1, 0) // 2
    lax.fori_loop(0, pairs, body, 0)

    @pl.when((qi > 0) & (qi % 2 == 1))
    def _():
        fold(qi - 1, parks[0])

    @pl.when((qi > 0) & (qi % 2 == 0))
    def _():
        score(qi - 1, parks[1])
        fold(qi - 2, parks[0])
        fold(qi - 1, parks[1])

    _write_heads(o_ref, l_scr, acc_scr)


def _fox_attention(qkv3, tproj, kx, c_t):
    batch, seq, _ = qkv3.shape
    t = CHUNK
    nq = seq // t
    return pl.pallas_call(
        _fox_kernel,
        grid=(batch, nq),
        in_specs=[
            pl.BlockSpec((None, seq, BRANCH_WIDTH), lambda b, i: (b, 0, COL_DK // 512),
                         pipeline_mode=pl.Buffered(1)),
            pl.BlockSpec((nq, BRANCH_WIDTH, t), lambda b, i: (b, ROW_DV // 512, 0),
                         pipeline_mode=pl.Buffered(1)),
            pl.BlockSpec((None, BRANCH_WIDTH, t), lambda b, i: (b * nq + i, ROW_DQ // 512, 0)),
            pl.BlockSpec((seq, 128), lambda b, i: (b, 0), pipeline_mode=pl.Buffered(1)),
            pl.BlockSpec((8, t), lambda b, i: (0, b * nq + i)),
        ],
        out_specs=pl.BlockSpec((None, t, BRANCH_WIDTH), lambda b, i: (b, i, 0)),
        out_shape=jax.ShapeDtypeStruct((batch, seq, BRANCH_WIDTH), BF16),
        scratch_shapes=[
            pltpu.VMEM((4, 256, t), BF16),
            pltpu.VMEM((nq, 4, 2 * t, 256), BF16),
        ] + [pltpu.VMEM((N_HEADS, t, t), F32), pltpu.VMEM((N_HEADS, 1, t), F32)] * 3 + [
            pltpu.VMEM((N_HEADS, 1, t), F32),
            pltpu.VMEM((N_HEADS, 1, t), F32),
            pltpu.VMEM((BRANCH_WIDTH, t), F32),
        ],
        compiler_params=_params(("parallel", "arbitrary"), 56),
        name="fox_attention",
    )(qkv3, tproj, tproj, kx, c_t)


COUNT_ROWS = 32
I16 = jnp.int16
I16_MIN = -32768


def _bf16_terms(value):
    out, rest = [], np.float32(value)
    for _ in range(3):
        bits = np.array([rest], np.float32).view(np.uint32)
        rounded = ((bits + 0x7FFF + ((bits >> 16) & 1)) & 0xFFFF0000).astype(np.uint32)
        term = rounded.view(np.float32)[0]
        out.append(float(term))
        rest = np.float32(rest - term)
    return out


SLOPES_C_LOG2 = [float(np.float32(s) * LOG2E) for s in SLOPES_C]
SLOPE_TERMS_C = [_bf16_terms(s) for s in SLOPES_C_LOG2]


def _dsa_kernel(k_ref, vt_ref, qt_ref, iqt_ref, ik_ref, iw_ref, o_ref,
                keys_scr, half_scr, w_scr, lhs_scr, s0_scr, mc0_scr, s1_scr, mc1_scr, m_scr, l_scr, acc_scr,
                *, topk):
    t = CHUNK
    qi = pl.program_id(1)
    n_chunks = qi + 1
    q0 = qi * t
    lane = lax.broadcasted_iota(I32, (t, 128), 1)
    lo = lane < HEAD_DIM
    qpos_row = q0 + lax.broadcasted_iota(I32, (1, t), 1)

    iws = [iw_ref[h:h + 1, :] for h in range(IDX_HEADS)]

    def score_chunk(j, carry):
        start = pl.multiple_of(j * t, t)
        ik = ik_ref[pl.ds(start, t), :]
        zeros = jnp.zeros_like(ik)
        ik_lo, ik_hi = jnp.where(lo, ik, zeros), jnp.where(lo, zeros, ik)
        lhs = jnp.concatenate(
            [jnp.concatenate([ik_lo, zeros], axis=1), jnp.concatenate([ik_hi, zeros], axis=1),
             jnp.concatenate([zeros, ik_lo], axis=1), jnp.concatenate([zeros, ik_hi], axis=1)], axis=0)
        dots = jnp.dot(lhs, iqt_ref[...], preferred_element_type=F32)
        score = jnp.zeros((t, t), F32)
        for h in range(IDX_HEADS):
            score = score + iws[h] * jnp.maximum(dots[h * t:(h + 1) * t, :], 0.0)
        bits = lax.bitcast_convert_type(score, I32)
        key = jnp.where(bits < 0, jnp.int32(INT_MIN) - bits, bits)
        kpos = start + lax.broadcasted_iota(I32, (t, 1), 0)
        key = jnp.where(kpos <= qpos_row, key, jnp.int32(INT_MIN))
        keys_scr[pl.ds(start, t), :] = key
        half_scr[pl.ds(start, t), :] = lax.shift_right_arithmetic(key, 16).astype(I16)
        return carry

    lax.fori_loop(0, n_chunks, score_chunk, 0)

    def count16(cand, strict):
        cand16 = cand.astype(I16)

        def body(j, acc):
            blk = half_scr[pl.ds(pl.multiple_of(j * t, t), t), :]
            hit = ((blk > cand16) if strict else (blk >= cand16)).reshape(t // COUNT_ROWS, COUNT_ROWS, t)
            for i in range(t // COUNT_ROWS):
                acc = jnp.where(hit[i], acc + jnp.int16(1), acc)
            return acc

        acc = lax.fori_loop(0, n_chunks, body, jnp.zeros((COUNT_ROWS, t), I16))
        return jnp.sum(acc.astype(I32), axis=0, keepdims=True)

    def bisect16(wanted):
        def bit_round(r, best):
            cand = best + lax.shift_left(jnp.int32(1), 15 - r)
            return jnp.where(count16(cand, False) >= wanted, cand, best)
        return lax.fori_loop(0, 16, bit_round, jnp.full((1, t), I16_MIN, I32))

    hi = bisect16(topk)
    above = count16(hi, True)

    def low_chunk(j, carry):
        start = pl.multiple_of(j * t, t)
        key = keys_scr[pl.ds(start, t), :]
        low = jnp.where(lax.shift_right_arithmetic(key, 16) == hi, key ^ 0x8000, 0x8000)
        half_scr[pl.ds(start, t), :] = low.astype(I16)
        return carry

    lax.fori_loop(0, n_chunks, low_chunk, 0)
    low = bisect16(topk - above)
    thr = lax.shift_left(hi, 16) + ((low + 0x8000) & 0xFFFF)
    ties_wanted = jnp.where(thr == INT_MIN, 0, topk - above - count16(low, True)).astype(F32)

    tpos = qpos_row.astype(F32)
    for h in range(N_HEADS):
        p, half = divmod(h, 2)
        neg_t = _split3(-(np.float32(SLOPES_C_LOG2[h]) * tpos))
        pieces = {}
        for i in range(3):
            pieces[i] = np.float32(64.0 * SLOPE_TERMS_C[h][i])
            pieces[3 + i] = np.float32(SLOPE_TERMS_C[h][i])
            pieces[6 + i] = neg_t[i].astype(F32)
        base = half * 128
        w_scr[p, base:base + HEAD_DIM, :] = qt_ref[h * HEAD_DIM:(h + 1) * HEAD_DIM, :]
        w_scr[p, base + HEAD_DIM:base + 128, :] = _row_select(HEAD_DIM, t, pieces).astype(BF16)

    krow = lax.broadcasted_iota(I32, (t, 128), 0)
    rel = lane - HEAD_DIM

    def key_extras(start):
        a = lax.shift_right_logical(start + krow, 6).astype(F32)
        b = (krow & 63).astype(F32)
        ex = jnp.where((rel >= 0) & (rel < 3), a, 0.0)
        ex = jnp.where((rel >= 3) & (rel < 6), b, ex)
        ex = jnp.where((rel >= 6) & (rel < 9), 1.0, ex)
        return ex.astype(BF16)

    @pl.when(qi == 0)
    def _():
        def build(j, carry):
            start = pl.multiple_of(j * t, t)
            k_aug = jnp.where(lo, k_ref[pl.ds(start, t), :], key_extras(start))
            zeros = jnp.zeros_like(k_aug)
            lhs_scr[j, 0:t, 0:128] = k_aug
            lhs_scr[j, 0:t, 128:256] = zeros
            lhs_scr[j, t:2 * t, 0:128] = zeros
            lhs_scr[j, t:2 * t, 128:256] = k_aug
            return carry

        lax.fori_loop(0, lhs_scr.shape[0], build, 0)

    _init_softmax(m_scr, l_scr, acc_scr)
    row = lax.broadcasted_iota(I32, (t, t), 0)
    col = lax.broadcasted_iota(I32, (t, t), 1)
    lower = jnp.where(col <= row, 1.0, 0.0).astype(BF16)

    parks = ((s0_scr, mc0_scr), (s1_scr, mc1_scr))

    def score(j, park, ties_seen):
        start = pl.multiple_of(j * t, t)
        key = keys_scr[pl.ds(start, t), :]
        eq = key == thr
        eqf = jnp.where(eq, 1.0, 0.0)
        prefix = jnp.dot(lower, eqf.astype(BF16), preferred_element_type=F32)
        rank = prefix - eqf + ties_seen
        sel = (key > thr) | (eq & (rank < ties_wanted))
        lhs = lhs_scr[j]
        for p in range(4):
            s2 = jnp.dot(lhs, w_scr[p], preferred_element_type=F32)
            for half in range(2):
                s = jnp.where(sel, s2[half * t:(half + 1) * t, :], NEG_INF)
                _stash_scores(s, park, 2 * p + half)
        return ties_seen + prefix[t - 1:t, :]

    def fold(j, park):
        vt = vt_ref[j]
        for h in range(N_HEADS):
            _fold_scores(park, h, vt, m_scr, l_scr, acc_scr)

    def body(i, ties_seen):
        ties_seen = score(2 * i + 1, parks[1], ties_seen)
        fold(2 * i, parks[0])
        ties_seen = score(2 * i + 2, parks[0], ties_seen)
        fold(2 * i + 1, parks[1])
        return ties_seen

    ties_seen = lax.fori_loop(0, qi // 2, body, score(0, parks[0], jnp.zeros((1, t), F32)))

    @pl.when(qi % 2 == 0)
    def _():
        fold(qi, parks[0])

    @pl.when(qi % 2 == 1)
    def _():
        score(qi, parks[1], ties_seen)
        fold(qi - 1, parks[0])
        fold(qi, parks[1])

    _write_heads(o_ref, l_scr, acc_scr)


def _dsa_attention(qkv3, tproj, misc):
    batch, seq, _ = qkv3.shape
    t = CHUNK
    nq = seq // t
    topk = min(TOPK_MAX, seq // 4)
    return pl.pallas_call(
        functools.partial(_dsa_kernel, topk=topk),
        grid=(batch, nq),
        in_specs=[
            pl.BlockSpec((None, seq, 128), lambda b, i: (b, 0, COL_CK // 128)),
            pl.BlockSpec((nq, HEAD_DIM, t), lambda b, i: (b, ROW_CV // HEAD_DIM, 0)),
            pl.BlockSpec((None, BRANCH_WIDTH, t), lambda b, i: (b * nq + i, ROW_CQ // 512, 0)),
            pl.BlockSpec((None, 256, t), lambda b, i: (b * nq + i, ROW_CIQ // 256, 0)),
            pl.BlockSpec((None, seq, 128), lambda b, i: (b, 0, COL_CIK // 128)),
            pl.BlockSpec((8, t), lambda b, i: (0, b * nq + i)),
        ],
        out_specs=pl.BlockSpec((None, t, BRANCH_WIDTH), lambda b, i: (b, i, 0)),
        out_shape=jax.ShapeDtypeStruct((batch, seq, BRANCH_WIDTH), BF16),
        scratch_shapes=[
            pltpu.VMEM((seq, t), I32),
            pltpu.VMEM((seq, t), I16),
            pltpu.VMEM((4, 256, t), BF16),
            pltpu.VMEM((nq, 2 * t, 256), BF16),
        ] + [pltpu.VMEM((N_HEADS, t, t), F32), pltpu.VMEM((N_HEADS, 1, t), F32)] * 2 + [
            pltpu.VMEM((N_HEADS, 1, t), F32),
            pltpu.VMEM((N_HEADS, 1, t), F32),
            pltpu.VMEM((BRANCH_WIDTH, t), F32),
        ],
        compiler_params=_params(("parallel", "arbitrary"), 48),
        name="dsa_attention",
    )(qkv3, tproj, tproj, tproj, qkv3, misc)


MERGE_TM = 512


def _interleave_classes(ref, dil, scr):
    n = MERGE_TM // dil
    for r in range(dil):
        for s in range(BRANCH_WIDTH // 128):
            scr[s, pl.ds(r, n, stride=dil), :] = ref[r, :, s * 128:(s + 1) * 128]
    return jnp.concatenate([scr[s] for s in range(BRANCH_WIDTH // 128)], axis=1)


def _merge_kernel(x_ref, oa_ref, ob1_ref, ob4_ref, ob16_ref, lw1_ref, lw4_ref, lw16_ref,
                  oc_ref, od_ref, wg_ref, bg_ref, wb_ref, wo_ref, g_ref, b_ref, out_ref, cls_scr):
    x = x_ref[...]
    xb = x.astype(BF16)
    ob4 = _interleave_classes(ob4_ref, 4, cls_scr.at[0])
    ob16 = _interleave_classes(ob16_ref, 16, cls_scr.at[1])
    lw4 = _interleave_classes(lw4_ref, 4, cls_scr.at[2])
    lw16 = _interleave_classes(lw16_ref, 16, cls_scr.at[3])
    lw1 = lw1_ref[...]
    top = jnp.maximum(jnp.maximum(lw1, lw4), lw16)
    w1, w4, w16 = jnp.exp2(lw1 - top), jnp.exp2(lw4 - top), jnp.exp2(lw16 - top)
    ob = (w1 * ob1_ref[...] + w4 * ob4 + w16 * ob16) / (w1 + w4 + w16)
    branches = (oa_ref[...], ob.astype(BF16), oc_ref[...], od_ref[...])
    merged = jnp.zeros((MERGE_TM, D_MODEL), F32)
    for n in range(4):
        sl = slice(n * D_MODEL, (n + 1) * D_MODEL)
        proj = jnp.dot(branches[n], wb_ref[n], preferred_element_type=F32)
        gate = jax.nn.sigmoid(jnp.dot(xb, wg_ref[:, sl], preferred_element_type=F32) + bg_ref[:, sl])
        merged = merged + gate * proj
    y = jnp.dot(merged.astype(BF16), wo_ref[...], preferred_element_type=F32)
    out_ref[...] = _layer_norm(DEEPNORM_ALPHA * x + y, g_ref[...], b_ref[...])


def _merge(x2d, oa, ob_parts, lw_parts, oc, od, wg, bg, wb, wo, g, b, seq):
    m = x2d.shape[0]
    tps = seq // MERGE_TM
    row = lambda w: pl.BlockSpec((MERGE_TM, w), lambda i: (i, 0))
    cls = lambda dil: pl.BlockSpec(
        (None, dil, MERGE_TM // dil, BRANCH_WIDTH), lambda i: (i // tps, 0, i % tps, 0))
    groups = [row(BRANCH_WIDTH), cls(4), cls(16)]
    return pl.pallas_call(
        _merge_kernel,
        grid=(m // MERGE_TM,),
        in_specs=[row(D_MODEL), row(BRANCH_WIDTH)] + groups + groups + [row(BRANCH_WIDTH)] * 2 + [
            _resident((D_MODEL, 4 * D_MODEL)),
            _resident((1, 4 * D_MODEL)),
            _resident((4, BRANCH_WIDTH, D_MODEL)),
            _resident((D_MODEL, D_MODEL)),
            _resident((1, D_MODEL)),
            _resident((1, D_MODEL)),
        ],
        out_specs=row(D_MODEL),
        out_shape=jax.ShapeDtypeStruct((m, D_MODEL), F32),
        scratch_shapes=[pltpu.VMEM((4, BRANCH_WIDTH // 128, MERGE_TM, 128), F32)],
        compiler_params=_params(("parallel",), 56),
        name="branch_merge",
    )(x2d, oa, *ob_parts, *lw_parts, oc, od, wg, bg, wb, wo, g, b)


FFN_TM = 512
FFN_CHUNK = 256
HALO = 8


def _gelu_tanh(a):
    return 0.5 * a * (1.0 + jnp.tanh(np.float32(np.sqrt(2.0 / np.pi)) * (a + 0.044715 * (a * a * a))))


def _ffn_kernel(x_ref, wu_ref, wg_ref, cw_ref, cb_ref, wd_ref, g_ref, b_ref, out_ref,
                a_scr, tail_scr, h_scr, *, tiles_per_seq):
    tm = FFN_TM
    x = x_ref[...]
    xb = x.astype(BF16)
    seq_start = (pl.program_id(0) % tiles_per_seq) == 0

    @pl.when(pl.program_id(0) == 0)
    def _():
        tail_scr[...] = jnp.zeros(tail_scr.shape, F32)

    for c in range(D_FF // FFN_CHUNK):
        sl = slice(c * FFN_CHUNK, (c + 1) * FFN_CHUNK)
        a = jnp.dot(xb, wu_ref[:, sl], preferred_element_type=F32)
        a_scr[0:HALO, :] = jnp.where(seq_start, 0.0, tail_scr[:, sl])
        a_scr[HALO:HALO + tm, :] = a
        tail_scr[:, sl] = a[tm - HALO:tm, :]
        conv = cb_ref[:, sl] + (cw_ref[0:1, sl] * a_scr[HALO - 2:HALO - 2 + tm, :]
                                + cw_ref[1:2, sl] * a_scr[HALO - 1:HALO - 1 + tm, :]
                                + cw_ref[2:3, sl] * a)
        gate = jnp.dot(xb, wg_ref[:, sl], preferred_element_type=F32)
        h_scr[:, sl] = (_gelu_tanh(conv) * gate).astype(BF16)
    y = jnp.dot(h_scr[...], wd_ref[...], preferred_element_type=F32)
    out_ref[...] = _layer_norm(DEEPNORM_ALPHA * x + y, g_ref[...], b_ref[...])


def _ffn(x2d, wu, wg, cw, cb, wd, g, b, seq):
    m = x2d.shape[0]
    tm = FFN_TM
    return pl.pallas_call(
        functools.partial(_ffn_kernel, tiles_per_seq=seq // tm),
        grid=(m // tm,),
        in_specs=[
            pl.BlockSpec((tm, D_MODEL), lambda i: (i, 0)),
            _resident((D_MODEL, D_FF)),
            _resident((D_MODEL, D_FF)),
            _resident((3, D_FF)),
            _resident((1, D_FF)),
            _resident((D_FF, D_MODEL)),
            _resident((1, D_MODEL)),
            _resident((1, D_MODEL)),
        ],
        out_specs=pl.BlockSpec((tm, D_MODEL), lambda i: (i, 0)),
        out_shape=jax.ShapeDtypeStruct((m, D_MODEL), F32),
        scratch_shapes=[pltpu.VMEM((HALO + tm, FFN_CHUNK), F32), pltpu.VMEM((HALO, D_FF), F32),
                        pltpu.VMEM((tm, D_FF), BF16)],
        compiler_params=_params(("arbitrary",), 56),
        name="conv_glu_ffn",
    )(x2d, wu, wg, cw, cb, wd, g, b)


def _prep_in_proj(w):
    offs = np.cumsum((0,) + IN_SPLIT_SIZES)
    (aq, ak, av, bq, bk, bv, cq, ck, cv, ciq, cik, ciw, dq, dk, dv, df) = [
        w[:, offs[i]:offs[i + 1]] for i in range(len(IN_SPLIT_SIZES))]
    scale = HEAD_DIM ** -0.5
    aq = aq.reshape(D_MODEL, N_HEADS, HEAD_DIM)[:, jnp.array(A_ORDER), :].reshape(D_MODEL, BRANCH_WIDTH)
    pad64 = jnp.zeros((D_MODEL, HEAD_DIM), F32)
    wcat = jnp.concatenate([aq * scale, bq * scale, bk, bv, dk, ak, av, ck, pad64, cik, cik], axis=1).astype(BF16)
    wmisc = jnp.concatenate([ciw, jnp.zeros((D_MODEL, 4), F32), df], axis=1).T.astype(BF16)
    wt = jnp.concatenate([dv, dq * scale, cq * scale, ciq * scale, cv], axis=1).T.astype(BF16)
    tscale = np.ones((T_ROWS, 1), np.float32)
    tscale[ROW_DQ:ROW_DQ + BRANCH_WIDTH] = LOG2E
    tscale[ROW_CQ:ROW_CQ + BRANCH_WIDTH] = LOG2E
    cscale = np.ones((1, PROJ_WIDTH), np.float32)
    cscale[:, COL_AQ:COL_AQ + BRANCH_WIDTH] = LOG2E
    cscale[:, COL_BQ:COL_BQ + BRANCH_WIDTH] = LOG2E
    return wcat, jnp.asarray(cscale), wmisc, wt, jnp.asarray(tscale)


def _layer(x2d, batch, seq, w_in, b_forget, sinks, w_branch, w_gate, b_gate, w_out, ln1_g, ln1_b,
           w_up, w_ffn_gate, conv_w, conv_b, w_down, ln2_g, ln2_b):
    m = batch * seq
    wcat, cscale, wmisc, wt, tscale = _prep_in_proj(w_in)
    qkv, misc, tproj, cls4, cls16 = _project(x2d, wcat, cscale, wmisc, wt, tscale, batch, seq)
    qkv3 = qkv.reshape(batch, seq, PROJ_WIDTH)
    classes = {4: cls4, 16: cls16}

    oa = _banded_attention(
        qkv3, COL_AQ // 512, COL_AK // 128, COL_AV // 128, 128, slopes=SLOPES_A, dist_scale=1,
        max_dist=A_WINDOW - 1, sinks=sinks, head_order=A_ORDER, out_dtype=BF16, name="swa_attention")

    ob_parts, lw_parts = [], []
    for window, dil in B_PATTERNS:
        if dil == 1:
            arr, cols = qkv3, (COL_BQ // 512, COL_BK // 512, COL_BV // 512)
        else:
            arr, cols = classes[dil].reshape(batch * dil, seq // dil, 3 * BRANCH_WIDTH), (0, 1, 2)
        o, lw = _banded_attention(
            arr, *cols, 512, slopes=SLOPES_B, dist_scale=dil, max_dist=window // dil, want_lw=True,
            name=f"dilated_attention_{dil}")
        shape = (m, BRANCH_WIDTH) if dil == 1 else (batch, dil, seq // dil, BRANCH_WIDTH)
        ob_parts.append(o.reshape(shape))
        lw_parts.append(lw.reshape(shape))

    oc = _dsa_attention(qkv3, tproj, misc)

    c_t, kx = _cum_forget(misc, b_forget, batch, seq)
    od = _fox_attention(qkv3, tproj, kx, c_t)

    wb = jnp.concatenate(
        [w_branch[0].reshape(N_HEADS, HEAD_DIM, D_MODEL)[jnp.array(A_ORDER)].reshape(1, BRANCH_WIDTH, D_MODEL),
         w_branch[1:]], axis=0).astype(BF16)
    x2d = _merge(
        x2d, oa.reshape(m, BRANCH_WIDTH), ob_parts, lw_parts, oc.reshape(m, BRANCH_WIDTH),
        od.reshape(m, BRANCH_WIDTH), w_gate.astype(BF16), b_gate.reshape(1, -1), wb, w_out.astype(BF16),
        ln1_g.reshape(1, -1), ln1_b.reshape(1, -1), seq)
    return _ffn(x2d, w_up.astype(BF16), w_ffn_gate.astype(BF16), conv_w, conv_b.reshape(1, -1),
                w_down.astype(BF16), ln2_g.reshape(1, -1), ln2_b.reshape(1, -1), seq)


def kernel(x, w_in, b_forget, sinks, w_branch, w_gate, b_gate, w_out, ln1_g, ln1_b,
           w_up, w_ffn_gate, conv_w, conv_b, w_down, ln2_g, ln2_b):
    batch, seq, _ = x.shape
    x2d = x.reshape(batch * seq, D_MODEL)
    for l in range(w_in.shape[0]):
        x2d = _layer(x2d, batch, seq, w_in[l], b_forget[l], sinks[l], w_branch[l], w_gate[l], b_gate[l],
                     w_out[l], ln1_g[l], ln1_b[l], w_up[l], w_ffn_gate[l], conv_w[l], conv_b[l],
                     w_down[l], ln2_g[l], ln2_b[l])
    return x2d.reshape(batch, seq, D_MODEL)
```

```python
import functools

import numpy as np
import jax
import jax.numpy as jnp
from jax import lax
from jax.experimental import pallas as pl
from jax.experimental.pallas import tpu as pltpu

F32 = jnp.float32
BF16 = jnp.bfloat16
I32 = jnp.int32

D_MODEL = 1024
HEAD_DIM = 64
N_HEADS = 8
BRANCH_WIDTH = N_HEADS * HEAD_DIM
BLOCK = 128
NEG_INF = -1e30
INT_MIN = -2147483648
LOG2E = np.float32(1.4426950408889634)
A_WINDOW = 128
B_PATTERNS = ((128, 1), (512, 4), (2048, 16))
IDX_HEADS = 4
TOPK_MAX = 256
D_FF = 2816
LN_EPS = 1e-5
DEPTH = 2
DEEPNORM_ALPHA = (2 * DEPTH) ** 0.25
IN_SPLIT_SIZES = (512, 128, 128, 512, 512, 512, 512, 64, 64, 256, 64, 4, 512, 512, 512, 8)

A_ORDER = (0, 4, 1, 5, 2, 6, 3, 7)
PLAIN_ORDER = tuple(range(N_HEADS))

COL_AQ, COL_BQ, COL_BK, COL_BV, COL_DK = (i * 512 for i in range(5))
COL_AK, COL_AV, COL_CK, COL_CIK = 2560, 2688, 2816, 2944
PROJ_WIDTH = 3072
MISC_ROWS = 16
ROW_DV, ROW_DQ, ROW_CQ, ROW_CIQ, ROW_CV = 0, 512, 1024, 1536, 1792
T_ROWS = 1856

CHUNK = 256


def _alibi_slopes():
    s = np.exp2(-8.0 * np.arange(1, 25, dtype=np.float32) / 24).astype(np.float32)
    return [float(v) for v in s[0::3]], [float(v) for v in s[1::3]], [float(v) for v in s[2::3]]


SLOPES_A, SLOPES_B, SLOPES_C = _alibi_slopes()


def _params(semantics, vmem_mib):
    return pltpu.CompilerParams(dimension_semantics=semantics, vmem_limit_bytes=vmem_mib * 2**20)


def _resident(shape):
    nd = len(shape)
    return pl.BlockSpec(shape, lambda *_: (0,) * nd, pipeline_mode=pl.Buffered(1))


def _nt_dot(a, b):
    return lax.dot_general(a, b, (((1,), (1,)), ((), ())), preferred_element_type=F32)


def _lane_lo(rows):
    return lax.broadcasted_iota(I32, (rows, 128), 1) < HEAD_DIM


def _split3(v):
    hi = v.astype(BF16)
    r1 = v - hi.astype(F32)
    mid = r1.astype(BF16)
    lo = (r1 - mid.astype(F32)).astype(BF16)
    return hi, mid, lo


def _layer_norm(z, g, b):
    mu = jnp.mean(z, axis=-1, keepdims=True)
    zc = z - mu
    var = jnp.mean(zc * zc, axis=-1, keepdims=True)
    return zc * lax.rsqrt(var + LN_EPS) * g + b


PROJ_TM = 512
PROJ_CHUNK = 256


B_SLABS = 3 * BRANCH_WIDTH // 128


def _proj_kernel(x_ref, w_ref, cs_ref, wm_ref, wt_ref, ts_ref, qkv_ref, misc_ref, t_ref, c4_ref, c16_ref,
                 b_scr):
    xb = x_ref[...].astype(BF16)
    for c in range(PROJ_WIDTH // PROJ_CHUNK):
        sl = slice(c * PROJ_CHUNK, (c + 1) * PROJ_CHUNK)
        acc = jnp.dot(xb, w_ref[:, sl], preferred_element_type=F32) * cs_ref[:, sl]
        qkv_ref[:, sl] = acc.astype(BF16)
        for half in range(PROJ_CHUNK // 128):
            col = c * PROJ_CHUNK + half * 128
            if COL_BQ <= col < COL_BV + BRANCH_WIDTH:
                b_scr[(col - COL_BQ) // 128] = acc[:, half * 128:(half + 1) * 128]
    misc_ref[...] = _nt_dot(wm_ref[...], xb)
    for c in range(PROJ_TM // CHUNK):
        acc = _nt_dot(wt_ref[...], xb[c * CHUNK:(c + 1) * CHUNK, :])
        t_ref[c] = (acc * ts_ref[...]).astype(BF16)
    for dil, ref in ((4, c4_ref), (16, c16_ref)):
        n = PROJ_TM // dil
        for r in range(dil):
            for s in range(B_SLABS):
                ref[r, :, s * 128:(s + 1) * 128] = b_scr[s, pl.ds(r, n, stride=dil), :].astype(BF16)


def _project(x2d, wcat, cscale, wmisc, wt, tscale, batch, seq):
    m = x2d.shape[0]
    tps = seq // PROJ_TM
    cls_spec = lambda dil: pl.BlockSpec(
        (None, dil, PROJ_TM // dil, 3 * BRANCH_WIDTH), lambda i: (i // tps, 0, i % tps, 0))
    cls_shape = lambda dil: jax.ShapeDtypeStruct((batch, dil, seq // dil, 3 * BRANCH_WIDTH), BF16)
    return pl.pallas_call(
        _proj_kernel,
        grid=(m // PROJ_TM,),
        in_specs=[
            pl.BlockSpec((PROJ_TM, D_MODEL), lambda i: (i, 0)),
            _resident((D_MODEL, PROJ_WIDTH)),
            _resident((1, PROJ_WIDTH)),
            _resident((MISC_ROWS, D_MODEL)),
            _resident((T_ROWS, D_MODEL)),
            _resident((T_ROWS, 1)),
        ],
        out_specs=[
            pl.BlockSpec((PROJ_TM, PROJ_WIDTH), lambda i: (i, 0)),
            pl.BlockSpec((MISC_ROWS, PROJ_TM), lambda i: (0, i)),
            pl.BlockSpec((PROJ_TM // CHUNK, T_ROWS, CHUNK), lambda i: (i, 0, 0)),
            cls_spec(4),
            cls_spec(16),
        ],
        out_shape=[
            jax.ShapeDtypeStruct((m, PROJ_WIDTH), BF16),
            jax.ShapeDtypeStruct((MISC_ROWS, m), F32),
            jax.ShapeDtypeStruct((m // CHUNK, T_ROWS, CHUNK), BF16),
            cls_shape(4),
            cls_shape(16),
        ],
        scratch_shapes=[pltpu.VMEM((B_SLABS, PROJ_TM, 128), F32)],
        compiler_params=_params(("parallel",), 48),
        name="in_proj",
    )(x2d, wcat, cscale, wmisc, wt, tscale)


KX_ONES = 24


def _cum_kernel(misc_ref, bf_ref, c_ref, kx_ref, *, seq):
    f = misc_ref[8:16, :] + bf_ref[...]
    ls = -(jnp.maximum(-f, 0.0) + jnp.log1p(jnp.exp(-jnp.abs(f))))
    row = lax.broadcasted_iota(I32, (128, 128), 0)
    col = lax.broadcasted_iota(I32, (128, 128), 1)
    upper = jnp.where(row <= col, 1.0, 0.0).astype(BF16)
    carry = jnp.zeros((8, 1), F32)
    ones = jnp.ones((8, 128), F32)
    pad = jnp.zeros((128 - 32, 128), F32)
    for j in range(seq // 128):
        hi, mid, lo = _split3(ls[:, j * 128:(j + 1) * 128])
        cs = (jnp.dot(hi, upper, preferred_element_type=F32)
              + jnp.dot(mid, upper, preferred_element_type=F32)
              + jnp.dot(lo, upper, preferred_element_type=F32))
        c = cs + carry
        c_ref[:, j * 128:(j + 1) * 128] = c
        carry = carry + cs[:, 127:128]
        nhi, nmid, nlo = _split3(-(c * LOG2E))
        stage = jnp.concatenate([nhi.astype(F32), nmid.astype(F32), nlo.astype(F32), ones, pad], axis=0)
        kx_ref[j * 128:(j + 1) * 128, :] = stage.T.astype(BF16)


def _cum_forget(misc, b_forget, batch, seq):
    return pl.pallas_call(
        functools.partial(_cum_kernel, seq=seq),
        grid=(batch,),
        in_specs=[
            pl.BlockSpec((MISC_ROWS, seq), lambda b: (0, b)),
            pl.BlockSpec((8, 1), lambda b: (0, 0)),
        ],
        out_specs=[
            pl.BlockSpec((8, seq), lambda b: (0, b)),
            pl.BlockSpec((seq, 128), lambda b: (b, 0)),
        ],
        out_shape=[
            jax.ShapeDtypeStruct((8, batch * seq), F32),
            jax.ShapeDtypeStruct((batch * seq, 128), BF16),
        ],
        compiler_params=_params(("parallel",), 32),
        name="forget_cumsum",
    )(misc, b_forget.reshape(8, 1))


BAND_BLOCKS = 4


def _np_split3(v):
    out, rest = [], np.asarray(v, np.float32)
    for _ in range(3):
        bits = rest.view(np.uint32)
        term = ((bits + 0x7FFF + ((bits >> 16) & 1)) & 0xFFFF0000).astype(np.uint32).view(np.float32)
        out.append(term)
        rest = (rest - term).astype(np.float32)
    return out


def _band_bias_features(slopes, dist_scale, head_order):
    qx = np.zeros((N_HEADS, BLOCK, 128), np.float32)
    for pos, h in enumerate(head_order):
        sl = np.float32(slopes[h]) * np.float32(dist_scale) * LOG2E
        for i, term in enumerate(_np_split3(np.full((BLOCK,), sl, np.float32))):
            qx[pos, :, i] = term
        for i, term in enumerate(_np_split3(-sl * (np.arange(BLOCK, dtype=np.float32) + BLOCK))):
            qx[pos, :, 3 + i] = term
    kx = np.zeros((2 * BLOCK, 128), np.float32)
    kx[:, 0:3] = np.arange(2 * BLOCK, dtype=np.float32)[:, None]
    kx[:, 3:6] = 1.0
    return jnp.asarray(qx, BF16), jnp.asarray(kx, BF16)


def _band_window(max_dist):
    qi = np.arange(2 * BLOCK)[:, None] % BLOCK
    ki = np.arange(2 * BLOCK)[None, :]
    dist = qi - ki + BLOCK
    inside = (dist >= 0) & (dist <= max_dist)
    return jnp.asarray(np.stack([inside & (ki >= BLOCK), inside]).astype(np.float32))


def _band_kernel(*refs, kv_width, n_blocks, has_sinks, head_order, want_lw):
    refs = list(refs)
    sink_ref = refs.pop(0) if has_sinks else None
    q_ref, kp_ref, kc_ref, vp_ref, vc_ref, qx_ref, kx_ref, win_ref = refs[:8]
    o_ref = refs[8]
    lw_ref = refs[9] if want_lw else None

    lo = _lane_lo(BLOCK)
    first_head = lax.broadcasted_iota(I32, (2 * BLOCK, 1), 0) < BLOCK
    kx = kx_ref[...]

    for blk in range(n_blocks):
        rows = slice(blk * BLOCK, (blk + 1) * BLOCK)
        if blk == 0:
            valid = win_ref[jnp.minimum(pl.program_id(1), 1)] > 0.5
        else:
            valid = win_ref[1] > 0.5
        for p in range(4):
            q2 = q_ref[rows, p * 128:(p + 1) * 128]
            zeros = jnp.zeros_like(q2)
            lhs = jnp.concatenate(
                [jnp.concatenate([jnp.where(lo, q2, zeros), qx_ref[2 * p]], axis=1),
                 jnp.concatenate([jnp.where(lo, zeros, q2), qx_ref[2 * p + 1]], axis=1)], axis=0)
            ksl = slice(0, 128) if kv_width == 128 else slice(p * 128, (p + 1) * 128)
            if blk == 0:
                k2 = jnp.concatenate([kp_ref[:, ksl], kc_ref[rows, ksl]], axis=0)
                v2 = jnp.concatenate([vp_ref[:, ksl], vc_ref[rows, ksl]], axis=0)
            else:
                k2 = kc_ref[(blk - 1) * BLOCK:(blk + 1) * BLOCK, ksl]
                v2 = vc_ref[(blk - 1) * BLOCK:(blk + 1) * BLOCK, ksl]
            s = _nt_dot(lhs, jnp.concatenate([k2, kx], axis=1))
            s = jnp.where(valid, s, NEG_INF)
            m = jnp.max(s, axis=1, keepdims=True)
            if has_sinks:
                sk = jnp.where(first_head, sink_ref[head_order[2 * p]], sink_ref[head_order[2 * p + 1]]) * LOG2E
                m = jnp.maximum(m, sk)
            pexp = jnp.exp2(s - m)
            l = jnp.sum(pexp, axis=1, keepdims=True)
            if has_sinks:
                l = l + jnp.exp2(sk - m)
            o2 = jnp.dot(pexp.astype(BF16), v2, preferred_element_type=F32) / l
            o_ref[rows, p * 128:(p + 1) * 128] = jnp.where(lo, o2[0:BLOCK], o2[BLOCK:2 * BLOCK]).astype(o_ref.dtype)
            if want_lw:
                lw = jnp.broadcast_to(m + jnp.log2(l), (2 * BLOCK, 128))
                lw_ref[rows, p * 128:(p + 1) * 128] = jnp.where(lo, lw[0:BLOCK], lw[BLOCK:2 * BLOCK])


def _banded_attention(arr, qcol, kcol, vcol, kv_width, *, slopes, dist_scale, max_dist,
                      sinks=None, head_order=PLAIN_ORDER, want_lw=False, out_dtype=F32, name):
    batch, length, _ = arr.shape
    n_blocks = min(BAND_BLOCKS, length // BLOCK)
    step = n_blocks * BLOCK
    kern = functools.partial(
        _band_kernel, kv_width=kv_width, n_blocks=n_blocks,
        has_sinks=sinks is not None, head_order=head_order, want_lw=want_lw)
    qx, kx = _band_bias_features(slopes, dist_scale, head_order)
    window = _band_window(max_dist)
    prev = lambda i: jnp.maximum(i * n_blocks - 1, 0)
    in_specs = [
        pl.BlockSpec((None, step, BRANCH_WIDTH), lambda b, i: (b, i, qcol)),
        pl.BlockSpec((None, BLOCK, kv_width), lambda b, i: (b, prev(i), kcol)),
        pl.BlockSpec((None, step, kv_width), lambda b, i: (b, i, kcol)),
        pl.BlockSpec((None, BLOCK, kv_width), lambda b, i: (b, prev(i), vcol)),
        pl.BlockSpec((None, step, kv_width), lambda b, i: (b, i, vcol)),
        _resident((N_HEADS, BLOCK, 128)),
        _resident((2 * BLOCK, 128)),
        _resident((2, 2 * BLOCK, 2 * BLOCK)),
    ]
    args = [arr, arr, arr, arr, arr, qx, kx, window]
    if sinks is not None:
        in_specs = [pl.BlockSpec(memory_space=pltpu.SMEM)] + in_specs
        args = [sinks] + args
    o_spec = pl.BlockSpec((None, step, BRANCH_WIDTH), lambda b, i: (b, i, 0))
    o_shape = jax.ShapeDtypeStruct((batch, length, BRANCH_WIDTH), out_dtype)
    if want_lw:
        out_specs = [o_spec, o_spec]
        out_shape = [o_shape, jax.ShapeDtypeStruct((batch, length, BRANCH_WIDTH), F32)]
    else:
        out_specs, out_shape = o_spec, o_shape
    return pl.pallas_call(
        kern, grid=(batch, length // step), in_specs=in_specs, out_specs=out_specs, out_shape=out_shape,
        compiler_params=_params(("parallel", "arbitrary"), 32), name=name,
    )(*args)


def _stash_scores(s, park, h):
    s_scr, mc_scr = park
    s_scr[h] = s
    mc_scr[h] = jnp.max(s, axis=0, keepdims=True)


def _fold_scores(park, h, vt, m_scr, l_scr, acc_scr):
    s_scr, mc_scr = park
    s = s_scr[h]
    m_prev = m_scr[h]
    m_new = jnp.maximum(m_prev, mc_scr[h])
    alpha = jnp.exp2(m_prev - m_new)
    pexp = jnp.exp2(s - m_new)
    l_scr[h] = alpha * l_scr[h] + jnp.sum(pexp, axis=0, keepdims=True)
    m_scr[h] = m_new
    rows = slice(h * HEAD_DIM, (h + 1) * HEAD_DIM)
    acc_scr[rows, :] = alpha * acc_scr[rows, :] + jnp.dot(vt, pexp.astype(BF16), preferred_element_type=F32)


def _init_softmax(m_scr, l_scr, acc_scr):
    m_scr[...] = jnp.full(m_scr.shape, NEG_INF, F32)
    l_scr[...] = jnp.zeros(l_scr.shape, F32)
    acc_scr[...] = jnp.zeros(acc_scr.shape, F32)


def _write_heads(o_ref, l_scr, acc_scr):
    t = CHUNK
    for p in range(4):
        l2 = jnp.concatenate(
            [jnp.broadcast_to(l_scr[2 * p], (HEAD_DIM, t)),
             jnp.broadcast_to(l_scr[2 * p + 1], (HEAD_DIM, t))], axis=0)
        o_ref[:, p * 128:(p + 1) * 128] = (acc_scr[p * 128:(p + 1) * 128, :] / l2).T.astype(o_ref.dtype)


def _row_select(rows, width, pieces):
    ridx = lax.broadcasted_iota(I32, (rows, width), 0)
    out = jnp.zeros((rows, width), F32)
    for r, piece in pieces.items():
        out = jnp.where(ridx == r, piece, out)
    return out


def _fox_kernel(k_ref, vt_ref, qt_ref, kx_ref, crow_ref, o_ref,
                w_scr, lhs_scr, s0_scr, mc0_scr, s1_scr, mc1_scr, s2_scr, mc2_scr, m_scr, l_scr, acc_scr):
    t = CHUNK
    qi = pl.program_id(1)
    lane = lax.broadcasted_iota(I32, (t, 128), 1)
    lo = lane < HEAD_DIM

    c_terms = [x.astype(F32) for x in _split3(crow_ref[...] * LOG2E)]
    ridx = lax.broadcasted_iota(I32, (128, t), 0)
    for p in range(4):
        pieces = {}
        for i in range(3):
            pieces[KX_ONES + i] = c_terms[i][2 * p:2 * p + 1, :]
            pieces[KX_ONES + 3 + i] = c_terms[i][2 * p + 1:2 * p + 2, :]
        ext = jnp.where(ridx < KX_ONES, 1.0, _row_select(128, t, pieces))
        w_scr[p, 0:128, :] = qt_ref[p * 128:(p + 1) * 128, :]
        w_scr[p, 128:256, :] = ext.astype(BF16)

    def bias_lanes(h, half):
        first = KX_ONES + 3 * half
        return (lane == h) | (lane == 8 + h) | (lane == 16 + h) | ((lane >= first) & (lane < first + 3))

    _init_softmax(m_scr, l_scr, acc_scr)
    causal = lax.broadcasted_iota(I32, (t, t), 0) <= lax.broadcasted_iota(I32, (t, t), 1)

    parks = ((s0_scr, mc0_scr), (s1_scr, mc1_scr), (s2_scr, mc2_scr))

    @pl.when(qi == 0)
    def _():
        def build(j, carry):
            start = pl.multiple_of(j * t, t)
            kx = kx_ref[pl.ds(start, t), :]
            zeros = jnp.zeros_like(kx)
            for p in range(4):
                k2 = k_ref[pl.ds(start, t), p * 128:(p + 1) * 128]
                lhs_scr[j, p, 0:t, 0:128] = jnp.where(lo, k2, zeros)
                lhs_scr[j, p, 0:t, 128:256] = jnp.where(bias_lanes(2 * p, 0), kx, zeros)
                lhs_scr[j, p, t:2 * t, 0:128] = jnp.where(lo, zeros, k2)
                lhs_scr[j, p, t:2 * t, 128:256] = jnp.where(bias_lanes(2 * p + 1, 1), kx, zeros)
            return carry

        lax.fori_loop(0, lhs_scr.shape[0], build, 0)

    def score(j, park, diagonal=False):
        for p in range(4):
            s2 = jnp.dot(lhs_scr[j, p], w_scr[p], preferred_element_type=F32)
            for half in range(2):
                s = s2[half * t:(half + 1) * t, :]
                if diagonal:
                    s = jnp.where(causal, s, NEG_INF)
                _stash_scores(s, park, 2 * p + half)

    def fold(j, park):
        vt = vt_ref[j]
        for h in range(N_HEADS):
            _fold_scores(park, h, vt[h * HEAD_DIM:(h + 1) * HEAD_DIM, :], m_scr, l_scr, acc_scr)

    score(qi, parks[2], diagonal=True)

    @pl.when(qi > 0)
    def _():
        score(0, parks[0])

    fold(qi, parks[2])

    def body(i, carry):
        score(2 * i + 1, parks[1])
        fold(2 * i, parks[0])
        score(2 * i + 2, parks[0])
        fold(2 * i + 1, parks[1])
        return carry

    pairs = jnp.maximum(qi - 1, 0) // 2
    lax.fori_loop(0, pairs, body, 0)

    @pl.when((qi > 0) & (qi % 2 == 1))
    def _():
        fold(qi - 1, parks[0])

    @pl.when((qi > 0) & (qi % 2 == 0))
    def _():
        score(qi - 1, parks[1])
        fold(qi - 2, parks[0])
        fold(qi - 1, parks[1])

    _write_heads(o_ref, l_scr, acc_scr)


def _fox_attention(qkv3, tproj, kx, c_t):
    batch, seq, _ = qkv3.shape
    t = CHUNK
    nq = seq // t
    return pl.pallas_call(
        _fox_kernel,
        grid=(batch, nq),
        in_specs=[
            pl.BlockSpec((None, seq, BRANCH_WIDTH), lambda b, i: (b, 0, COL_DK // 512),
                         pipeline_mode=pl.Buffered(1)),
            pl.BlockSpec((nq, BRANCH_WIDTH, t), lambda b, i: (b, ROW_DV // 512, 0),
                         pipeline_mode=pl.Buffered(1)),
            pl.BlockSpec((None, BRANCH_WIDTH, t), lambda b, i: (b * nq + i, ROW_DQ // 512, 0)),
            pl.BlockSpec((seq, 128), lambda b, i: (b, 0), pipeline_mode=pl.Buffered(1)),
            pl.BlockSpec((8, t), lambda b, i: (0, b * nq + i)),
        ],
        out_specs=pl.BlockSpec((None, t, BRANCH_WIDTH), lambda b, i: (b, i, 0)),
        out_shape=jax.ShapeDtypeStruct((batch, seq, BRANCH_WIDTH), BF16),
        scratch_shapes=[
            pltpu.VMEM((4, 256, t), BF16),
            pltpu.VMEM((nq, 4, 2 * t, 256), BF16),
        ] + [pltpu.VMEM((N_HEADS, t, t), F32), pltpu.VMEM((N_HEADS, 1, t), F32)] * 3 + [
            pltpu.VMEM((N_HEADS, 1, t), F32),
            pltpu.VMEM((N_HEADS, 1, t), F32),
            pltpu.VMEM((BRANCH_WIDTH, t), F32),
        ],
        compiler_params=_params(("parallel", "arbitrary"), 56),
        name="fox_attention",
    )(qkv3, tproj, tproj, kx, c_t)


COUNT_ROWS = 32
I16 = jnp.int16
I16_MIN = -32768


def _bf16_terms(value):
    out, rest = [], np.float32(value)
    for _ in range(3):
        bits = np.array([rest], np.float32).view(np.uint32)
        rounded = ((bits + 0x7FFF + ((bits >> 16) & 1)) & 0xFFFF0000).astype(np.uint32)
        term = rounded.view(np.float32)[0]
        out.append(float(term))
        rest = np.float32(rest - term)
    return out


SLOPES_C_LOG2 = [float(np.float32(s) * LOG2E) for s in SLOPES_C]
SLOPE_TERMS_C = [_bf16_terms(s) for s in SLOPES_C_LOG2]


def _dsa_kernel(k_ref, vt_ref, qt_ref, iqt_ref, ik_ref, iw_ref, o_ref,
                keys_scr, half_scr, w_scr, lhs_scr, s0_scr, mc0_scr, s1_scr, mc1_scr, m_scr, l_scr, acc_scr,
                *, topk):
    t = CHUNK
    qi = pl.program_id(1)
    n_chunks = qi + 1
    q0 = qi * t
    lane = lax.broadcasted_iota(I32, (t, 128), 1)
    lo = lane < HEAD_DIM
    qpos_row = q0 + lax.broadcasted_iota(I32, (1, t), 1)

    iws = [iw_ref[h:h + 1, :] for h in range(IDX_HEADS)]

    def score_chunk(j):
        start = pl.multiple_of(j * t, t)
        ik = ik_ref[pl.ds(start, t), :]
        zeros = jnp.zeros_like(ik)
        lhs = jnp.concatenate([jnp.where(lo, ik, zeros), jnp.where(lo, zeros, ik)], axis=0)
        dots = (jnp.dot(lhs, iqt_ref[0:128, :], preferred_element_type=F32),
                jnp.dot(lhs, iqt_ref[128:256, :], preferred_element_type=F32))
        score = jnp.zeros((t, t), F32)
        for h in range(IDX_HEADS):
            score = score + iws[h] * jnp.maximum(dots[h // 2][(h % 2) * t:(h % 2 + 1) * t, :], 0.0)
        bits = lax.bitcast_convert_type(score, I32)
        key = jnp.where(bits < 0, jnp.int32(INT_MIN) - bits, bits)
        kpos = start + lax.broadcasted_iota(I32, (t, 1), 0)
        key = jnp.where(kpos <= qpos_row, key, jnp.int32(INT_MIN))
        keys_scr[pl.ds(start, t), :] = key
        half_scr[pl.ds(start, t), :] = lax.shift_right_arithmetic(key, 16).astype(I16)

    def score_pair(i, carry):
        score_chunk(2 * i)
        score_chunk(2 * i + 1)
        return carry

    lax.fori_loop(0, (n_chunks + 1) // 2, score_pair, 0)

    def count16(cand, strict):
        cand16 = cand.astype(I16)

        def body(j, acc):
            blk = half_scr[pl.ds(pl.multiple_of(j * t, t), t), :]
            hit = ((blk > cand16) if strict else (blk >= cand16)).reshape(t // COUNT_ROWS, COUNT_ROWS, t)
            for i in range(t // COUNT_ROWS):
                acc = jnp.where(hit[i], acc + jnp.int16(1), acc)
            return acc

        acc = lax.fori_loop(0, n_chunks, body, jnp.zeros((COUNT_ROWS, t), I16))
        return jnp.sum(acc.astype(I32), axis=0, keepdims=True)

    def bisect16(wanted):
        def bit_round(r, best):
            cand = best + lax.shift_left(jnp.int32(1), 15 - r)
            return jnp.where(count16(cand, False) >= wanted, cand, best)
        return lax.fori_loop(0, 16, bit_round, jnp.full((1, t), I16_MIN, I32))

    hi = bisect16(topk)
    above = count16(hi, True)

    def low_chunk(j, carry):
        start = pl.multiple_of(j * t, t)
        key = keys_scr[pl.ds(start, t), :]
        low = jnp.where(lax.shift_right_arithmetic(key, 16) == hi, key ^ 0x8000, 0x8000)
        half_scr[pl.ds(start, t), :] = low.astype(I16)
        return carry

    lax.fori_loop(0, n_chunks, low_chunk, 0)
    low = bisect16(topk - above)
    thr = lax.shift_left(hi, 16) + ((low + 0x8000) & 0xFFFF)
    ties_wanted = jnp.where(thr == INT_MIN, 0, topk - above - count16(low, True)).astype(F32)

    tpos = qpos_row.astype(F32)
    for h in range(N_HEADS):
        p, half = divmod(h, 2)
        neg_t = _split3(-(np.float32(SLOPES_C_LOG2[h]) * tpos))
        pieces = {}
        for i in range(3):
            pieces[i] = np.float32(64.0 * SLOPE_TERMS_C[h][i])
            pieces[3 + i] = np.float32(SLOPE_TERMS_C[h][i])
            pieces[6 + i] = neg_t[i].astype(F32)
        base = half * 128
        w_scr[p, base:base + HEAD_DIM, :] = qt_ref[h * HEAD_DIM:(h + 1) * HEAD_DIM, :]
        w_scr[p, base + HEAD_DIM:base + 128, :] = _row_select(HEAD_DIM, t, pieces).astype(BF16)

    krow = lax.broadcasted_iota(I32, (t, 128), 0)
    rel = lane - HEAD_DIM

    def key_extras(start):
        a = lax.shift_right_logical(start + krow, 6).astype(F32)
        b = (krow & 63).astype(F32)
        ex = jnp.where((rel >= 0) & (rel < 3), a, 0.0)
        ex = jnp.where((rel >= 3) & (rel < 6), b, ex)
        ex = jnp.where((rel >= 6) & (rel < 9), 1.0, ex)
        return ex.astype(BF16)

    @pl.when(qi == 0)
    def _():
        def build(j, carry):
            start = pl.multiple_of(j * t, t)
            k_aug = jnp.where(lo, k_ref[pl.ds(start, t), :], key_extras(start))
            zeros = jnp.zeros_like(k_aug)
            lhs_scr[j, 0:t, 0:128] = k_aug
            lhs_scr[j, 0:t, 128:256] = zeros
            lhs_scr[j, t:2 * t, 0:128] = zeros
            lhs_scr[j, t:2 * t, 128:256] = k_aug
            return carry

        lax.fori_loop(0, lhs_scr.shape[0], build, 0)

    _init_softmax(m_scr, l_scr, acc_scr)
    row = lax.broadcasted_iota(I32, (t, t), 0)
    col = lax.broadcasted_iota(I32, (t, t), 1)
    lower = jnp.where(col <= row, 1.0, 0.0).astype(BF16)

    parks = ((s0_scr, mc0_scr), (s1_scr, mc1_scr))

    def score(j, park, ties_seen):
        start = pl.multiple_of(j * t, t)
        key = keys_scr[pl.ds(start, t), :]
        eq = key == thr
        eqf = jnp.where(eq, 1.0, 0.0)
        prefix = jnp.dot(lower, eqf.astype(BF16), preferred_element_type=F32)
        rank = prefix - eqf + ties_seen
        sel = (key > thr) | (eq & (rank < ties_wanted))
        lhs = lhs_scr[j]
        for p in range(4):
            s2 = jnp.dot(lhs, w_scr[p], preferred_element_type=F32)
            for half in range(2):
                s = jnp.where(sel, s2[half * t:(half + 1) * t, :], NEG_INF)
                _stash_scores(s, park, 2 * p + half)
        return ties_seen + prefix[t - 1:t, :]

    def fold(j, park):
        vt = vt_ref[j]
        for h in range(N_HEADS):
            _fold_scores(park, h, vt, m_scr, l_scr, acc_scr)

    def body(i, ties_seen):
        ties_seen = score(2 * i + 1, parks[1], ties_seen)
        fold(2 * i, parks[0])
        ties_seen = score(2 * i + 2, parks[0], ties_seen)
        fold(2 * i + 1, parks[1])
        return ties_seen

    ties_seen = lax.fori_loop(0, qi // 2, body, score(0, parks[0], jnp.zeros((1, t), F32)))

    @pl.when(qi % 2 == 0)
    def _():
        fold(qi, parks[0])

    @pl.when(qi % 2 == 1)
    def _():
        score(qi, parks[1], ties_seen)
        fold(qi - 1, parks[0])
        fold(qi, parks[1])

    _write_heads(o_ref, l_scr, acc_scr)


def _dsa_attention(qkv3, tproj, misc):
    batch, seq, _ = qkv3.shape
    t = CHUNK
    nq = seq // t
    topk = min(TOPK_MAX, seq // 4)
    assert nq % 2 == 0, "the indexer scores key chunks in pairs"
    return pl.pallas_call(
        functools.partial(_dsa_kernel, topk=topk),
        grid=(batch, nq),
        in_specs=[
            pl.BlockSpec((None, seq, 128), lambda b, i: (b, 0, COL_CK // 128)),
            pl.BlockSpec((nq, HEAD_DIM, t), lambda b, i: (b, ROW_CV // HEAD_DIM, 0)),
            pl.BlockSpec((None, BRANCH_WIDTH, t), lambda b, i: (b * nq + i, ROW_CQ // 512, 0)),
            pl.BlockSpec((None, 256, t), lambda b, i: (b * nq + i, ROW_CIQ // 256, 0)),
            pl.BlockSpec((None, seq, 128), lambda b, i: (b, 0, COL_CIK // 128)),
            pl.BlockSpec((8, t), lambda b, i: (0, b * nq + i)),
        ],
        out_specs=pl.BlockSpec((None, t, BRANCH_WIDTH), lambda b, i: (b, i, 0)),
        out_shape=jax.ShapeDtypeStruct((batch, seq, BRANCH_WIDTH), BF16),
        scratch_shapes=[
            pltpu.VMEM((seq, t), I32),
            pltpu.VMEM((seq, t), I16),
            pltpu.VMEM((4, 256, t), BF16),
            pltpu.VMEM((nq, 2 * t, 256), BF16),
        ] + [pltpu.VMEM((N_HEADS, t, t), F32), pltpu.VMEM((N_HEADS, 1, t), F32)] * 2 + [
            pltpu.VMEM((N_HEADS, 1, t), F32),
            pltpu.VMEM((N_HEADS, 1, t), F32),
            pltpu.VMEM((BRANCH_WIDTH, t), F32),
        ],
        compiler_params=_params(("parallel", "arbitrary"), 48),
        name="dsa_attention",
    )(qkv3, tproj, tproj, tproj, qkv3, misc)


MERGE_TM = 512
MERGE_PARTS = 2


def _interleave_classes(ref, dil, scr):
    n = MERGE_TM // dil
    for r in range(dil):
        for s in range(BRANCH_WIDTH // 128):
            scr[s, pl.ds(r, n, stride=dil), :] = ref[r, :, s * 128:(s + 1) * 128]
    return jnp.concatenate([scr[s] for s in range(BRANCH_WIDTH // 128)], axis=1)


def _merge_kernel(x_ref, oa_ref, ob1_ref, ob4_ref, ob16_ref, lw1_ref, lw4_ref, lw16_ref,
                  oc_ref, od_ref, wg_ref, bg_ref, wb_ref, wo_ref, g_ref, b_ref, out_ref, cls_scr):
    x = x_ref[...]
    xb = x.astype(BF16)
    ob4 = _interleave_classes(ob4_ref, 4, cls_scr.at[0])
    ob16 = _interleave_classes(ob16_ref, 16, cls_scr.at[1])
    lw4 = _interleave_classes(lw4_ref, 4, cls_scr.at[2])
    lw16 = _interleave_classes(lw16_ref, 16, cls_scr.at[3])
    lw1 = lw1_ref[...]
    top = jnp.maximum(jnp.maximum(lw1, lw4), lw16)
    w1, w4, w16 = jnp.exp2(lw1 - top), jnp.exp2(lw4 - top), jnp.exp2(lw16 - top)
    ob = ((w1 * ob1_ref[...] + w4 * ob4 + w16 * ob16) / (w1 + w4 + w16)).astype(BF16)
    for part in range(MERGE_PARTS):
        rows = slice(part * MERGE_TM // MERGE_PARTS, (part + 1) * MERGE_TM // MERGE_PARTS)
        branches = (oa_ref[rows, :], ob[rows, :], oc_ref[rows, :], od_ref[rows, :])
        merged = jnp.zeros((MERGE_TM // MERGE_PARTS, D_MODEL), F32)
        for n in range(4):
            sl = slice(n * D_MODEL, (n + 1) * D_MODEL)
            proj = jnp.dot(branches[n], wb_ref[n], preferred_element_type=F32)
            gate = jax.nn.sigmoid(jnp.dot(xb[rows, :], wg_ref[:, sl], preferred_element_type=F32) + bg_ref[:, sl])
            merged = merged + gate * proj
        y = jnp.dot(merged.astype(BF16), wo_ref[...], preferred_element_type=F32)
        out_ref[rows, :] = _layer_norm(DEEPNORM_ALPHA * x[rows, :] + y, g_ref[...], b_ref[...])


def _merge(x2d, oa, ob_parts, lw_parts, oc, od, wg, bg, wb, wo, g, b, seq):
    m = x2d.shape[0]
    tps = seq // MERGE_TM
    row = lambda w: pl.BlockSpec((MERGE_TM, w), lambda i: (i, 0))
    cls = lambda dil: pl.BlockSpec(
        (None, dil, MERGE_TM // dil, BRANCH_WIDTH), lambda i: (i // tps, 0, i % tps, 0))
    groups = [row(BRANCH_WIDTH), cls(4), cls(16)]
    return pl.pallas_call(
        _merge_kernel,
        grid=(m // MERGE_TM,),
        in_specs=[row(D_MODEL), row(BRANCH_WIDTH)] + groups + groups + [row(BRANCH_WIDTH)] * 2 + [
            _resident((D_MODEL, 4 * D_MODEL)),
            _resident((1, 4 * D_MODEL)),
            _resident((4, BRANCH_WIDTH, D_MODEL)),
            _resident((D_MODEL, D_MODEL)),
            _resident((1, D_MODEL)),
            _resident((1, D_MODEL)),
        ],
        out_specs=row(D_MODEL),
        out_shape=jax.ShapeDtypeStruct((m, D_MODEL), F32),
        scratch_shapes=[pltpu.VMEM((4, BRANCH_WIDTH // 128, MERGE_TM, 128), F32)],
        compiler_params=_params(("parallel",), 56),
        name="branch_merge",
    )(x2d, oa, *ob_parts, *lw_parts, oc, od, wg, bg, wb, wo, g, b)


FFN_TM = 512
FFN_PARTS = 2
FFN_CHUNK = 256
HALO = 8


def _gelu_tanh(a):
    return 0.5 * a * (1.0 + jnp.tanh(np.float32(np.sqrt(2.0 / np.pi)) * (a + 0.044715 * (a * a * a))))


def _ffn_kernel(x_ref, wu_ref, wg_ref, cw_ref, cb_ref, wd_ref, g_ref, b_ref, out_ref,
                a_scr, tail_scr, h_scr, *, tiles_per_seq):
    tm = FFN_TM
    x = x_ref[...]
    xb = x.astype(BF16)
    seq_start = (pl.program_id(0) % tiles_per_seq) == 0

    @pl.when(pl.program_id(0) == 0)
    def _():
        tail_scr[...] = jnp.zeros(tail_scr.shape, F32)

    for c in range(D_FF // FFN_CHUNK):
        sl = slice(c * FFN_CHUNK, (c + 1) * FFN_CHUNK)
        a = jnp.dot(xb, wu_ref[:, sl], preferred_element_type=F32)
        a_scr[0:HALO, :] = jnp.where(seq_start, 0.0, tail_scr[:, sl])
        a_scr[HALO:HALO + tm, :] = a
        tail_scr[:, sl] = a[tm - HALO:tm, :]
        conv = cb_ref[:, sl] + (cw_ref[0:1, sl] * a_scr[HALO - 2:HALO - 2 + tm, :]
                                + cw_ref[1:2, sl] * a_scr[HALO - 1:HALO - 1 + tm, :]
                                + cw_ref[2:3, sl] * a)
        gate = jnp.dot(xb, wg_ref[:, sl], preferred_element_type=F32)
        h_scr[:, sl] = (_gelu_tanh(conv) * gate).astype(BF16)
    for part in range(FFN_PARTS):
        rows = slice(part * tm // FFN_PARTS, (part + 1) * tm // FFN_PARTS)
        y = jnp.dot(h_scr[rows, :], wd_ref[...], preferred_element_type=F32)
        out_ref[rows, :] = _layer_norm(DEEPNORM_ALPHA * x[rows, :] + y, g_ref[...], b_ref[...])


def _ffn(x2d, wu, wg, cw, cb, wd, g, b, seq):
    m = x2d.shape[0]
    tm = FFN_TM
    return pl.pallas_call(
        functools.partial(_ffn_kernel, tiles_per_seq=seq // tm),
        grid=(m // tm,),
        in_specs=[
            pl.BlockSpec((tm, D_MODEL), lambda i: (i, 0)),
            _resident((D_MODEL, D_FF)),
            _resident((D_MODEL, D_FF)),
            _resident((3, D_FF)),
            _resident((1, D_FF)),
            _resident((D_FF, D_MODEL)),
            _resident((1, D_MODEL)),
            _resident((1, D_MODEL)),
        ],
        out_specs=pl.BlockSpec((tm, D_MODEL), lambda i: (i, 0)),
        out_shape=jax.ShapeDtypeStruct((m, D_MODEL), F32),
        scratch_shapes=[pltpu.VMEM((HALO + tm, FFN_CHUNK), F32), pltpu.VMEM((HALO, D_FF), F32),
                        pltpu.VMEM((tm, D_FF), BF16)],
        compiler_params=_params(("arbitrary",), 56),
        name="conv_glu_ffn",
    )(x2d, wu, wg, cw, cb, wd, g, b)


def _prep_in_proj(w):
    offs = np.cumsum((0,) + IN_SPLIT_SIZES)
    (aq, ak, av, bq, bk, bv, cq, ck, cv, ciq, cik, ciw, dq, dk, dv, df) = [
        w[:, offs[i]:offs[i + 1]] for i in range(len(IN_SPLIT_SIZES))]
    scale = HEAD_DIM ** -0.5
    aq = aq.reshape(D_MODEL, N_HEADS, HEAD_DIM)[:, jnp.array(A_ORDER), :].reshape(D_MODEL, BRANCH_WIDTH)
    pad64 = jnp.zeros((D_MODEL, HEAD_DIM), F32)
    wcat = jnp.concatenate([aq * scale, bq * scale, bk, bv, dk, ak, av, ck, pad64, cik, cik], axis=1).astype(BF16)
    wmisc = jnp.concatenate([ciw, jnp.zeros((D_MODEL, 4), F32), df], axis=1).T.astype(BF16)
    wt = jnp.concatenate([dv, dq * scale, cq * scale, ciq * scale, cv], axis=1).T.astype(BF16)
    tscale = np.ones((T_ROWS, 1), np.float32)
    tscale[ROW_DQ:ROW_DQ + BRANCH_WIDTH] = LOG2E
    tscale[ROW_CQ:ROW_CQ + BRANCH_WIDTH] = LOG2E
    cscale = np.ones((1, PROJ_WIDTH), np.float32)
    cscale[:, COL_AQ:COL_AQ + BRANCH_WIDTH] = LOG2E
    cscale[:, COL_BQ:COL_BQ + BRANCH_WIDTH] = LOG2E
    return wcat, jnp.asarray(cscale), wmisc, wt, jnp.asarray(tscale)


def _layer(x2d, batch, seq, w_in, b_forget, sinks, w_branch, w_gate, b_gate, w_out, ln1_g, ln1_b,
           w_up, w_ffn_gate, conv_w, conv_b, w_down, ln2_g, ln2_b):
    m = batch * seq
    wcat, cscale, wmisc, wt, tscale = _prep_in_proj(w_in)
    qkv, misc, tproj, cls4, cls16 = _project(x2d, wcat, cscale, wmisc, wt, tscale, batch, seq)
    qkv3 = qkv.reshape(batch, seq, PROJ_WIDTH)
    classes = {4: cls4, 16: cls16}

    oa = _banded_attention(
        qkv3, COL_AQ // 512, COL_AK // 128, COL_AV // 128, 128, slopes=SLOPES_A, dist_scale=1,
        max_dist=A_WINDOW - 1, sinks=sinks, head_order=A_ORDER, out_dtype=BF16, name="swa_attention")

    ob_parts, lw_parts = [], []
    for window, dil in B_PATTERNS:
        if dil == 1:
            arr, cols = qkv3, (COL_BQ // 512, COL_BK // 512, COL_BV // 512)
        else:
            arr, cols = classes[dil].reshape(batch * dil, seq // dil, 3 * BRANCH_WIDTH), (0, 1, 2)
        o, lw = _banded_attention(
            arr, *cols, 512, slopes=SLOPES_B, dist_scale=dil, max_dist=window // dil, want_lw=True,
            name=f"dilated_attention_{dil}")
        shape = (m, BRANCH_WIDTH) if dil == 1 else (batch, dil, seq // dil, BRANCH_WIDTH)
        ob_parts.append(o.reshape(shape))
        lw_parts.append(lw.reshape(shape))

    oc = _dsa_attention(qkv3, tproj, misc)

    c_t, kx = _cum_forget(misc, b_forget, batch, seq)
    od = _fox_attention(qkv3, tproj, kx, c_t)

    wb = jnp.concatenate(
        [w_branch[0].reshape(N_HEADS, HEAD_DIM, D_MODEL)[jnp.array(A_ORDER)].reshape(1, BRANCH_WIDTH, D_MODEL),
         w_branch[1:]], axis=0).astype(BF16)
    x2d = _merge(
        x2d, oa.reshape(m, BRANCH_WIDTH), ob_parts, lw_parts, oc.reshape(m, BRANCH_WIDTH),
        od.reshape(m, BRANCH_WIDTH), w_gate.astype(BF16), b_gate.reshape(1, -1), wb, w_out.astype(BF16),
        ln1_g.reshape(1, -1), ln1_b.reshape(1, -1), seq)
    return _ffn(x2d, w_up.astype(BF16), w_ffn_gate.astype(BF16), conv_w, conv_b.reshape(1, -1),
                w_down.astype(BF16), ln2_g.reshape(1, -1), ln2_b.reshape(1, -1), seq)


def kernel(x, w_in, b_forget, sinks, w_branch, w_gate, b_gate, w_out, ln1_g, ln1_b,
           w_up, w_ffn_gate, conv_w, conv_b, w_down, ln2_g, ln2_b):
    batch, seq, _ = x.shape
    x2d = x.reshape(batch * seq, D_MODEL)
    for l in range(w_in.shape[0]):
        x2d = _layer(x2d, batch, seq, w_in[l], b_forget[l], sinks[l], w_branch[l], w_gate[l], b_gate[l],
                     w_out[l], ln1_g[l], ln1_b[l], w_up[l], w_ffn_gate[l], conv_w[l], conv_b[l],
                     w_down[l], ln2_g[l], ln2_b[l])
    return x2d.reshape(batch, seq, D_MODEL)
```
